```python
import jax, jax.numpy as jnp
from jax import lax
import numpy as np

D_MODEL = 2048
BATCH = 2
SEQ = 4096
DEPTH = 1

CONV_WIDTH = 4
GDN_HEADS = 16
GDN_HEAD_DIM = 128
GDN_WIDTH = GDN_HEADS * GDN_HEAD_DIM
GDN_CHUNK = 64
SSM_HEAD_DIM = 64
SSM_INNER = D_MODEL
SSM_HEADS = SSM_INNER // SSM_HEAD_DIM
SSM_GROUPS = 8
SSM_STATE = 128
SSM_CONV_CH = SSM_INNER + 2 * SSM_GROUPS * SSM_STATE
SSM_CHUNK = 128
MLP_HIDDEN = 4 * D_MODEL
EPS = 1e-6
IN_SIZES = (3 * GDN_WIDTH,
            GDN_WIDTH,
            GDN_HEADS,
            GDN_HEADS,
            SSM_INNER,
            SSM_CONV_CH,
            SSM_HEADS,
            D_MODEL,
            D_MODEL)
IN_TOTAL = sum(IN_SIZES)

kernel_name = "hybrid_gdn_mamba2_gated_merge_block"


def _rms_f32(x):
    return x * lax.rsqrt(jnp.mean(x * x, axis=-1, keepdims=True) + EPS)


def rms_norm(x, w):
    xf = x.astype(jnp.float32)
    return (_rms_f32(xf) * w.astype(jnp.float32)).astype(x.dtype)


def causal_depthwise_conv(x, w, b=None):
    K, C = w.shape
    y = lax.conv_general_dilated(x, w[:, None, :].astype(x.dtype), window_strides=(1,),
                                 padding=[(K - 1, 0)],
                                 dimension_numbers=('NWC', 'WIO', 'NWC'),
                                 feature_group_count=C)
    if b is not None:
        y = y + b.astype(x.dtype)
    return y


def l2norm(x):
    return x * lax.rsqrt(jnp.sum(x * x, axis=-1, keepdims=True) + EPS)


def _to_chunks(a, n, c):
    Bsz, T, H = a.shape[:3]
    return jnp.moveaxis(a.reshape(Bsz, n, c, H, *a.shape[3:]), 3, 1)


def gated_delta_rule_chunked(q, k, v, g, beta, chunk):
    Bsz, T, H, K = q.shape
    V = v.shape[-1]
    n = T // chunk
    q = _to_chunks(q * (K ** -0.5), n, chunk)
    k = _to_chunks(k, n, chunk)
    v = _to_chunks(v, n, chunk)
    beta = _to_chunks(beta, n, chunk)
    g = jnp.cumsum(_to_chunks(g, n, chunk), axis=-1)
    causal = jnp.tril(jnp.ones((chunk, chunk), dtype=bool))
    strict = jnp.tril(jnp.ones((chunk, chunk), dtype=bool), -1)
    diff = g[..., :, None] - g[..., None, :]
    decay = jnp.where(causal, jnp.exp(jnp.where(causal, diff, 0.0)), 0.0)
    k_beta = k * beta[..., None]
    v_beta = v * beta[..., None]
    Lmat = jnp.where(strict, jnp.einsum('bhncd,bhnsd->bhncs', k_beta, k) * decay, 0.0)
    eye = jnp.eye(chunk, dtype=q.dtype)
    rhs = jnp.concatenate([v_beta, k_beta * jnp.exp(g)[..., None]], axis=-1)
    sol = lax.linalg.triangular_solve(eye + Lmat, rhs, left_side=True, lower=True)
    u, w = sol[..., :V], sol[..., V:]
    attn = jnp.where(causal, jnp.einsum('bhncd,bhnsd->bhncs', q, k) * decay, 0.0)
    q_dec = q * jnp.exp(g)[..., None]
    g_last = g[..., -1]
    k_tail = k * jnp.exp(g_last[..., None] - g)[..., None]

    def step(S, inp):
        qd, wc, uc, at, kt, gl = inp
        v_new = uc - jnp.einsum('bhck,bhkv->bhcv', wc, S)
        o = jnp.einsum('bhck,bhkv->bhcv', qd, S) + jnp.einsum('bhcs,bhsv->bhcv', at, v_new)
        S = S * jnp.exp(gl)[..., None, None] + jnp.einsum('bhck,bhcv->bhkv', kt, v_new)
        return S, o

    xs = (jnp.moveaxis(q_dec, 2, 0), jnp.moveaxis(w, 2, 0), jnp.moveaxis(u, 2, 0),
          jnp.moveaxis(attn, 2, 0), jnp.moveaxis(k_tail, 2, 0), jnp.moveaxis(g_last, 2, 0))
    S0 = jnp.zeros((Bsz, H, K, V), dtype=q.dtype)
    _, o = lax.scan(step, S0, xs)
    return o.transpose(1, 0, 3, 2, 4).reshape(Bsz, T, H, V)


def ssd_chunked(x, dt, A, Bm, Cm, chunk):
    Bsz, T, H, P = x.shape
    G, N = Bm.shape[2], Bm.shape[3]
    R = H // G
    n = T // chunk
    xdt = (x * dt[..., None]).reshape(Bsz, n, chunk, G, R, P)
    a = (dt * A).reshape(Bsz, n, chunk, G, R).transpose(0, 3, 4, 1, 2)
    a_cum = jnp.cumsum(a, axis=-1)
    Bc = Bm.reshape(Bsz, n, chunk, G, N)
    Cc = Cm.reshape(Bsz, n, chunk, G, N)
    causal = jnp.tril(jnp.ones((chunk, chunk), dtype=bool))
    seg = a_cum[..., :, None] - a_cum[..., None, :]
    Lmask = jnp.where(causal, jnp.exp(jnp.where(causal, seg, 0.0)), 0.0)
    scores = jnp.einsum('bclgn,bcsgn->bgcls', Cc, Bc)
    y_diag = jnp.einsum('bgrcls,bcsgrp->bclgrp', scores[:, :, None] * Lmask, xdt)
    decay_states = jnp.exp(a_cum[..., -1:] - a_cum)
    states = jnp.einsum('bclgn,bgrcl,bclgrp->bcgrpn', Bc, decay_states, xdt)
    chunk_decay = jnp.exp(a_cum[..., -1])

    def step(h, inp):
        s, d = inp
        return h * d[..., None, None] + s, h

    h0 = jnp.zeros((Bsz, G, R, P, N), dtype=x.dtype)
    _, h_in = lax.scan(step, h0, (jnp.moveaxis(states, 1, 0), jnp.moveaxis(chunk_decay, 3, 0)))
    y_off = jnp.einsum('bclgn,cbgrpn,bgrcl->bclgrp', Cc, h_in, jnp.exp(a_cum))
    return (y_diag + y_off).reshape(Bsz, T, H, P)


def hybrid_mixer(xn, w_in, gdn_conv_w, gdn_a_log, gdn_dt_bias, gdn_norm_w,
                 ssm_conv_w, ssm_conv_b, ssm_a_log, ssm_dt_bias, ssm_d, ssm_norm_w,
                 w_gdn_out, w_ssm_out, w_o):
    f32 = jnp.float32
    Bsz, T, _ = xn.shape
    proj = xn @ w_in
    splits, acc = [], 0
    for s in IN_SIZES[:-1]:
        acc += s
        splits.append(acc)
    qkv, gdn_z, gdn_b, gdn_a, ssm_z, xbc, ssm_dt, gate_a, gate_b = jnp.split(proj, splits, axis=-1)

    qkv = jax.nn.silu(causal_depthwise_conv(qkv, gdn_conv_w)).astype(f32)
    q, k, v = jnp.split(qkv, 3, axis=-1)
    q = l2norm(q.reshape(Bsz, T, GDN_HEADS, GDN_HEAD_DIM))
    k = l2norm(k.reshape(Bsz, T, GDN_HEADS, GDN_HEAD_DIM))
    v = v.reshape(Bsz, T, GDN_HEADS, GDN_HEAD_DIM)
    beta = jax.nn.sigmoid(gdn_b.astype(f32))
    g = -jnp.exp(gdn_a_log.astype(f32)) * jax.nn.softplus(gdn_a.astype(f32) + gdn_dt_bias.astype(f32))
    o = gated_delta_rule_chunked(q, k, v, g, beta, GDN_CHUNK)
    z = gdn_z.astype(f32).reshape(Bsz, T, GDN_HEADS, GDN_HEAD_DIM)
    o = _rms_f32(o) * gdn_norm_w.astype(f32) * jax.nn.silu(z)
    o = o.reshape(Bsz, T, GDN_WIDTH).astype(xn.dtype)

    xbc = jax.nn.silu(causal_depthwise_conv(xbc, ssm_conv_w, ssm_conv_b)).astype(f32)
    xs, Bm, Cm = jnp.split(xbc, [SSM_INNER, SSM_INNER + SSM_GROUPS * SSM_STATE], axis=-1)
    xs = xs.reshape(Bsz, T, SSM_HEADS, SSM_HEAD_DIM)
    Bm = Bm.reshape(Bsz, T, SSM_GROUPS, SSM_STATE)
    Cm = Cm.reshape(Bsz, T, SSM_GROUPS, SSM_STATE)
    dt = jax.nn.softplus(ssm_dt.astype(f32) + ssm_dt_bias.astype(f32))
    A = -jnp.exp(ssm_a_log.astype(f32))
    y = ssd_chunked(xs, dt, A, Bm, Cm, SSM_CHUNK) + ssm_d.astype(f32)[:, None] * xs
    y = y.reshape(Bsz, T, SSM_INNER) * jax.nn.silu(ssm_z.astype(f32))
    y = _rms_f32(y.reshape(Bsz, T, SSM_GROUPS, SSM_INNER // SSM_GROUPS)).reshape(Bsz, T, SSM_INNER)
    y = (y * ssm_norm_w.astype(f32)).astype(xn.dtype)

    merged = jax.nn.sigmoid(gate_a) * (o @ w_gdn_out) + jax.nn.sigmoid(gate_b) * (y @ w_ssm_out)
    return merged @ w_o


def sq_relu_mlp(xn, w_up, w_down):
    return jnp.square(jax.nn.relu(xn @ w_up)) @ w_down


def setup_inputs(seed: int = 0) -> dict:
    key = jax.random.key(seed)
    ks = jax.random.split(key, 24)
    L, D = DEPTH, D_MODEL

    def nrm(k, shape, scale):
        return jax.random.normal(k, shape, jnp.float32) * scale

    def dt_bias(k, shape):
        u = jax.random.uniform(k, shape, jnp.float32)
        dt = jnp.exp(u * (np.log(0.1) - np.log(0.001)) + np.log(0.001))
        return dt + jnp.log(-jnp.expm1(-dt))

    def a_log(k, shape):
        return jnp.log(jax.random.uniform(k, shape, jnp.float32, 1.0, 16.0))

    return {
        "x": nrm(ks[0], (BATCH, SEQ, D), 1.0),
        "norm1_w": 1.0 + nrm(ks[1], (L, D), 0.02),
        "w_in": nrm(ks[2], (L, D, IN_TOTAL), D ** -0.5),
        "gdn_conv_w": nrm(ks[3], (L, CONV_WIDTH, 3 * GDN_WIDTH), CONV_WIDTH ** -0.5),
        "gdn_a_log": a_log(ks[4], (L, GDN_HEADS)),
        "gdn_dt_bias": dt_bias(ks[5], (L, GDN_HEADS)),
        "gdn_norm_w": 1.0 + nrm(ks[6], (L, GDN_HEAD_DIM), 0.02),
        "ssm_conv_w": nrm(ks[7], (L, CONV_WIDTH, SSM_CONV_CH), CONV_WIDTH ** -0.5),
        "ssm_conv_b": nrm(ks[8], (L, SSM_CONV_CH), 0.02),
        "ssm_a_log": a_log(ks[9], (L, SSM_HEADS)),
        "ssm_dt_bias": dt_bias(ks[10], (L, SSM_HEADS)),
        "ssm_d": 1.0 + nrm(ks[11], (L, SSM_HEADS), 0.1),
        "ssm_norm_w": 1.0 + nrm(ks[12], (L, SSM_INNER), 0.02),
        "w_gdn_out": nrm(ks[13], (L, GDN_WIDTH, D), GDN_WIDTH ** -0.5),
        "w_ssm_out": nrm(ks[14], (L, SSM_INNER, D), SSM_INNER ** -0.5),
        "w_o": nrm(ks[15], (L, D, D), D ** -0.5),
        "norm2_w": 1.0 + nrm(ks[16], (L, D), 0.02),
        "w_up": nrm(ks[17], (L, D, MLP_HIDDEN), D ** -0.5),
        "w_down": nrm(ks[18], (L, MLP_HIDDEN, D), MLP_HIDDEN ** -0.5),
        "final_norm_w": 1.0 + nrm(ks[19], (D,), 0.02),
    }


def reference(x, norm1_w, w_in, gdn_conv_w, gdn_a_log, gdn_dt_bias, gdn_norm_w,
              ssm_conv_w, ssm_conv_b, ssm_a_log, ssm_dt_bias, ssm_d, ssm_norm_w,
              w_gdn_out, w_ssm_out, w_o, norm2_w, w_up, w_down, final_norm_w):
    h = x
    for l in range(DEPTH):
        h = h + hybrid_mixer(rms_norm(h, norm1_w[l]), w_in[l], gdn_conv_w[l], gdn_a_log[l],
                             gdn_dt_bias[l], gdn_norm_w[l], ssm_conv_w[l], ssm_conv_b[l],
                             ssm_a_log[l], ssm_dt_bias[l], ssm_d[l], ssm_norm_w[l],
                             w_gdn_out[l], w_ssm_out[l], w_o[l])
        h = h + sq_relu_mlp(rms_norm(h, norm2_w[l]), w_up[l], w_down[l])
    return rms_norm(h, final_norm_w)
```

```python
import functools

import jax
import jax.numpy as jnp
from jax import lax
from jax.experimental import pallas as pl
from jax.experimental.pallas import tpu as pltpu

F32 = jnp.float32
BF16 = jnp.bfloat16
HIGHEST = lax.Precision.HIGHEST
EPS = 1e-6

LANES = 128
CONV_K = 4
GDN_HEADS = 16
GDN_DK = 128
GDN_CHUNK = 64
GDN_HB = 4
SSM_HEADS = 32
SSM_P = 64
SSM_GROUPS = 8
SSM_N = 128
SSM_CHUNK = 128
SSM_R = SSM_HEADS // SSM_GROUPS
MIX_TB = 512
VMEM_LIMIT = 48 * 1024 * 1024

LANE_BETA = 0
LANE_GDEC = 16
LANE_DT = 32


def _sigmoid(x):
    return 1.0 / (1.0 + jnp.exp(-x))


def _silu(x):
    return x * _sigmoid(x)


def _dot(a, b, precision=None):
    return jnp.dot(a, b, preferred_element_type=F32, precision=precision)


def _dot_nt(a, b):
    return lax.dot_general(a, b, (((1,), (1,)), ((), ())), preferred_element_type=F32)


def _dot_tn(a, b):
    return lax.dot_general(a, b, (((0,), (0,)), ((), ())), preferred_element_type=F32)


def _inproj_kernel(x_ref, nw_ref, wm_ref, ws_ref, pm_ref, ps_ref, xn_scr):
    @pl.when(pl.program_id(1) == 0)
    def _():
        x = x_ref[...]
        xn = x * lax.rsqrt(jnp.mean(x * x, axis=-1, keepdims=True) + EPS) * nw_ref[...]
        xn_scr[...] = xn.astype(BF16)
        ps_ref[...] = _dot(xn_scr[...], ws_ref[...])

    pm_ref[...] = _dot(xn_scr[...], wm_ref[...])


def _inproj(x2, norm_w, w_main, w_small, tm=1024, tn=1024):
    bt, d = x2.shape
    n = w_main.shape[1]
    return pl.pallas_call(
        _inproj_kernel,
        grid=(bt // tm, n // tn),
        in_specs=[
            pl.BlockSpec((tm, d), lambda i, j: (i, 0)),
            pl.BlockSpec((1, d), lambda i, j: (0, 0)),
            pl.BlockSpec((d, tn), lambda i, j: (0, j)),
            pl.BlockSpec((d, LANES), lambda i, j: (0, 0)),
        ],
        out_specs=[
            pl.BlockSpec((tm, tn), lambda i, j: (i, j)),
            pl.BlockSpec((tm, LANES), lambda i, j: (i, 0)),
        ],
        out_shape=[
            jax.ShapeDtypeStruct((bt, n), F32),
            jax.ShapeDtypeStruct((bt, LANES), F32),
        ],
        scratch_shapes=[pltpu.VMEM((tm, d), BF16)],
        compiler_params=pltpu.CompilerParams(
            dimension_semantics=("parallel", "arbitrary"),
            vmem_limit_bytes=VMEM_LIMIT),
        name="inproj",
    )(x2, norm_w, w_main, w_small)


def _smallprep_kernel(s_ref, p_ref, g_ref, m_ref):
    tm = s_ref.shape[0]
    bias = p_ref[0:1, :]
    neg_a = -jnp.exp(p_ref[1:2, :])
    ii = lax.broadcasted_iota(jnp.int32, (LANES, LANES), 0)
    jj = lax.broadcasted_iota(jnp.int32, (LANES, LANES), 1)
    same64 = (ii // GDN_CHUNK) == (jj // GDN_CHUNK)
    tril128 = jnp.where(ii >= jj, 1.0, 0.0).astype(F32)
    ones64 = jnp.where(same64, 1.0, 0.0).astype(F32)
    tril64 = tril128 * ones64
    ones128 = jnp.ones((LANES, LANES), F32)
    for s in range(tm // LANES):
        rows = slice(s * LANES, (s + 1) * LANES)
        x = s_ref[rows, :]
        xb = x + bias
        sp = jnp.maximum(xb, 0.0) + jnp.log1p(jnp.exp(-jnp.abs(xb)))
        dec = neg_a * sp
        c64 = _dot(tril64, dec, HIGHEST)
        l64 = _dot(ones64, dec, HIGHEST)
        c128 = _dot(tril128, dec, HIGHEST)
        l128 = _dot(ones128, dec, HIGHEST)
        g_ref[0, rows, :] = _sigmoid(x)
        g_ref[1, rows, :] = c64
        g_ref[2, rows, :] = jnp.exp(c64)
        g_ref[3, rows, :] = jnp.exp(l64 - c64)
        g_ref[4, rows, :] = jnp.exp(l64)
        m_ref[0, rows, :] = sp
        m_ref[1, rows, :] = c128
        m_ref[2, rows, :] = jnp.exp(c128)
        m_ref[3, rows, :] = jnp.exp(l128 - c128)
        m_ref[4, rows, :] = jnp.exp(l128)


def _smallprep(small, params, tm=512):
    bt = small.shape[0]
    spec = pl.BlockSpec((5, tm, LANES), lambda i: (0, i, 0))
    shp = jax.ShapeDtypeStruct((5, bt, LANES), F32)
    return pl.pallas_call(
        _smallprep_kernel,
        grid=(bt // tm,),
        in_specs=[pl.BlockSpec((tm, LANES), lambda i: (i, 0)),
                  pl.BlockSpec((8, LANES), lambda i: (0, 0))],
        out_specs=[spec, spec],
        out_shape=[shp, shp],
        compiler_params=pltpu.CompilerParams(dimension_semantics=("parallel",)),
        name="smallprep",
    )(small, params)


def _conv_silu(xe_scr, r0, rows, w_ref, b_ref=None):
    win = xe_scr[pl.ds(r0, rows + 8), :]
    acc = win[8:] * w_ref[CONV_K - 1:CONV_K, :]
    for k in range(CONV_K - 1):
        acc = acc + pltpu.roll(win, CONV_K - 1 - k, axis=0)[8:] * w_ref[k:k + 1, :]
    if b_ref is not None:
        acc = acc + b_ref[...]
    return _silu(acc)


def _col(field, lane_iota, lane):
    return jnp.sum(jnp.where(lane_iota == lane, field, 0.0), axis=-1, keepdims=True)


def _gdn_kernel(q_ref, k_ref, v_ref, z_ref, cwq_ref, cwk_ref, cwv_ref, sm_ref, smt_ref, nw_ref,
                o_ref,
                xq_scr, xk_scr, xv_scr, tail_scr, s_scr, w_scr, u_scr, qg_scr, kd_scr, at_scr,
                egl_scr):
    tb = pl.program_id(1)
    hg = pl.program_id(2)
    tbs = q_ref.shape[0]
    nch = tbs // GDN_CHUNK
    c = GDN_CHUNK
    dk = GDN_DK

    @pl.when(tb == 0)
    def _():
        tail_scr[:, hg] = jnp.zeros((3,) + tail_scr.shape[2:], F32)
        s_scr[pl.ds(hg * GDN_HB, GDN_HB)] = jnp.zeros((GDN_HB, dk, dk), F32)

    for idx, (src, scr) in enumerate(((q_ref, xq_scr), (k_ref, xk_scr), (v_ref, xv_scr))):
        scr[0:8, :] = tail_scr[idx, hg]
        scr[8:8 + tbs, :] = src[...]
        tail_scr[idx, hg] = src[tbs - 8:tbs, :]

    ii = lax.broadcasted_iota(jnp.int32, (c, c), 0)
    jj = lax.broadcasted_iota(jnp.int32, (c, c), 1)
    causal = ii >= jj
    strict = ii > jj
    eye = jnp.where(ii == jj, 1.0, 0.0).astype(F32)
    lane_iota = lax.broadcasted_iota(jnp.int32, (c, LANES), 1)
    qscale = dk ** -0.5

    def phase_a(ci, carry):
        r0 = pl.multiple_of(ci * c, c)
        rows = pl.ds(r0, c)
        qc = _conv_silu(xq_scr, r0, c, cwq_ref)
        kc = _conv_silu(xk_scr, r0, c, cwk_ref)
        vc = _conv_silu(xv_scr, r0, c, cwv_ref)
        sm = sm_ref[:, rows, :]
        for r in range(GDN_HB):
            h = hg * GDN_HB + r
            cols = slice(r * dk, (r + 1) * dk)
            beta_c = _col(sm[0], lane_iota, LANE_BETA + h)
            gc_c = _col(sm[1], lane_iota, LANE_GDEC + h)
            egc_c = _col(sm[2], lane_iota, LANE_GDEC + h)
            ekl_c = _col(sm[3], lane_iota, LANE_GDEC + h)
            egl_c = _col(sm[4], lane_iota, LANE_GDEC + h)
            beta_r = smt_ref[0, ci, pl.ds(LANE_BETA + h, 1), :]
            gc_r = smt_ref[1, ci, pl.ds(LANE_GDEC + h, 1), :]
            egc_r = smt_ref[2, ci, pl.ds(LANE_GDEC + h, 1), :]
            q = qc[:, cols]
            k = kc[:, cols]
            v = vc[:, cols]
            qn = q * lax.rsqrt(jnp.sum(q * q, axis=-1, keepdims=True) + EPS) * qscale
            kn = k * lax.rsqrt(jnp.sum(k * k, axis=-1, keepdims=True) + EPS)
            kb = kn.astype(BF16)
            gram = _dot_nt(jnp.concatenate([kb, qn.astype(BF16)], axis=0), kb)
            diff = gc_c - gc_r
            dec = jnp.where(causal, jnp.exp(jnp.where(causal, diff, 0.0)), 0.0)
            a_mat = jnp.where(strict, gram[:c] * beta_c * dec, 0.0)
            attn = gram[c:] * dec
            m = -a_mat
            t_inv = eye + m
            for _ in range(5):
                m = _dot(m, m, HIGHEST)
                t_inv = t_inv + _dot(t_inv, m, HIGHEST)
            u = _dot((t_inv * beta_r).astype(BF16), v.astype(BF16))
            w = _dot((t_inv * (beta_r * egc_r)).astype(BF16), kb)
            w_scr[r, rows, :] = w.astype(BF16)
            u_scr[r, rows, :] = u
            qg_scr[r, rows, :] = (qn * egc_c).astype(BF16)
            kd_scr[r, rows, :] = (kn * ekl_c).astype(BF16)
            at_scr[r, rows, :] = attn.astype(BF16)
            egl_scr[r, rows, :] = jnp.broadcast_to(egl_c, (c, dk))
        return carry

    lax.fori_loop(0, nch, phase_a, 0)

    nw = nw_ref[...]

    def phase_b(ci, carry):
        r0 = pl.multiple_of(ci * c, c)
        rows = pl.ds(r0, c)
        for r in range(GDN_HB):
            h = hg * GDN_HB + r
            cols = slice(r * dk, (r + 1) * dk)
            s = s_scr[h]
            wq = jnp.concatenate([w_scr[r, rows, :], qg_scr[r, rows, :]], axis=0)
            res = _dot(wq, s.astype(BF16))
            v_new = u_scr[r, rows, :] - res[:c]
            vb = v_new.astype(BF16)
            o = res[c:] + _dot(at_scr[r, rows, :], vb)
            egl = egl_scr[r, rows, :]
            s_scr[h] = s * jnp.concatenate([egl, egl], axis=0) + _dot_tn(kd_scr[r, rows, :], vb)
            z = z_ref[rows, cols]
            on = o * lax.rsqrt(jnp.mean(o * o, axis=-1, keepdims=True) + EPS) * nw * _silu(z)
            o_ref[rows, cols] = on.astype(o_ref.dtype)
        return carry

    lax.fori_loop(0, nch, phase_b, 0)


def _gdn(proj, conv_w, g_fields, g_rows, norm_w, batch, seq, col0_qkv, col0_z):
    bt = proj.shape[0]
    tbs = MIX_TB
    ntb = seq // tbs
    nhg = GDN_HEADS // GDN_HB
    cw = GDN_HB * GDN_DK
    width = GDN_HEADS * GDN_DK
    qb, kb_, vb, zb = (col0_qkv // cw, (col0_qkv + width) // cw, (col0_qkv + 2 * width) // cw,
                       col0_z // cw)
    row = lambda b, t, h: b * ntb + t
    return pl.pallas_call(
        _gdn_kernel,
        grid=(batch, ntb, nhg),
        in_specs=[
            pl.BlockSpec((tbs, cw), lambda b, t, h: (row(b, t, h), qb + h)),
            pl.BlockSpec((tbs, cw), lambda b, t, h: (row(b, t, h), kb_ + h)),
            pl.BlockSpec((tbs, cw), lambda b, t, h: (row(b, t, h), vb + h)),
            pl.BlockSpec((tbs, cw), lambda b, t, h: (row(b, t, h), zb + h)),
            pl.BlockSpec((CONV_K, cw), lambda b, t, h: (0, h)),
            pl.BlockSpec((CONV_K, cw), lambda b, t, h: (0, nhg + h)),
            pl.BlockSpec((CONV_K, cw), lambda b, t, h: (0, 2 * nhg + h)),
            pl.BlockSpec((5, tbs, LANES), lambda b, t, h: (0, row(b, t, h), 0)),
            pl.BlockSpec((3, tbs // GDN_CHUNK, LANES, GDN_CHUNK),
                         lambda b, t, h: (0, row(b, t, h), 0, 0)),
            pl.BlockSpec((1, GDN_DK), lambda b, t, h: (0, 0)),
        ],
        out_specs=pl.BlockSpec((tbs, cw), lambda b, t, h: (row(b, t, h), h)),
        out_shape=jax.ShapeDtypeStruct((bt, width), BF16),
        scratch_shapes=[
            pltpu.VMEM((tbs + 8, cw), F32),
            pltpu.VMEM((tbs + 8, cw), F32),
            pltpu.VMEM((tbs + 8, cw), F32),
            pltpu.VMEM((3, nhg, 8, cw), F32),
            pltpu.VMEM((GDN_HEADS, GDN_DK, GDN_DK), F32),
            pltpu.VMEM((GDN_HB, tbs, GDN_DK), BF16),
            pltpu.VMEM((GDN_HB, tbs, GDN_DK), F32),
            pltpu.VMEM((GDN_HB, tbs, GDN_DK), BF16),
            pltpu.VMEM((GDN_HB, tbs, GDN_DK), BF16),
            pltpu.VMEM((GDN_HB, tbs, GDN_CHUNK), BF16),
            pltpu.VMEM((GDN_HB, tbs, GDN_DK), F32),
        ],
        compiler_params=pltpu.CompilerParams(
            dimension_semantics=("arbitrary", "arbitrary", "arbitrary"),
            vmem_limit_bytes=VMEM_LIMIT),
        name="gdn",
    )(proj, proj, proj, proj, conv_w, conv_w, conv_w, g_fields, g_rows, norm_w)


def _ssd_kernel(x_ref, b_ref, c_ref, z_ref, cwx_ref, cwb_ref, cwc_ref, cbx_ref, cbb_ref, cbc_ref,
                sm_ref, smt_ref, dsk_ref, nw_ref,
                o_ref,
                xx_scr, xb_scr, xc_scr, tailx_scr, tailb_scr, tailc_scr, h_scr):
    tb = pl.program_id(1)
    g = pl.program_id(2)
    tbs = x_ref.shape[0]
    l = SSM_CHUNK
    p = SSM_P
    nch = tbs // l

    @pl.when(tb == 0)
    def _():
        tailx_scr[g] = jnp.zeros(tailx_scr.shape[1:], F32)
        tailb_scr[g] = jnp.zeros(tailb_scr.shape[1:], F32)
        tailc_scr[g] = jnp.zeros(tailc_scr.shape[1:], F32)
        h_scr[g] = jnp.zeros(h_scr.shape[1:], F32)

    for src, scr, tail in ((x_ref, xx_scr, tailx_scr), (b_ref, xb_scr, tailb_scr),
                           (c_ref, xc_scr, tailc_scr)):
        scr[0:8, :] = tail[g]
        scr[8:8 + tbs, :] = src[...]
        tail[g] = src[tbs - 8:tbs, :]

    ii = lax.broadcasted_iota(jnp.int32, (l, l), 0)
    jj = lax.broadcasted_iota(jnp.int32, (l, l), 1)
    causal = ii >= jj
    lane_iota = lax.broadcasted_iota(jnp.int32, (l, LANES), 1)
    dsk = dsk_ref[...]
    nw = nw_ref[...]

    def body(ci, carry):
        r0 = pl.multiple_of(ci * l, l)
        rows = pl.ds(r0, l)
        xc = _conv_silu(xx_scr, r0, l, cwx_ref, cbx_ref)
        bc = _conv_silu(xb_scr, r0, l, cwb_ref, cbb_ref).astype(BF16)
        cc = _conv_silu(xc_scr, r0, l, cwc_ref, cbc_ref).astype(BF16)
        scores = _dot_nt(cc, bc)
        sm = sm_ref[:, rows, :]
        h_in = h_scr[g]
        y_off = _dot(cc, h_in.astype(BF16))
        ys, xds, eals = [], [], []
        for r in range(SSM_R):
            lane = LANE_DT + g * SSM_R + r
            cols = slice(r * p, (r + 1) * p)
            dt_c = _col(sm[0], lane_iota, lane)
            ac_c = _col(sm[1], lane_iota, lane)
            eac_c = _col(sm[2], lane_iota, lane)
            ds_c = _col(sm[3], lane_iota, lane)
            eal_c = _col(sm[4], lane_iota, lane)
            ac_r = smt_ref[ci, pl.ds(lane, 1), :]
            lmask = jnp.where(causal, jnp.exp(jnp.where(causal, ac_c - ac_r, 0.0)), 0.0)
            xh = xc[:, cols]
            xdt = xh * dt_c
            y_diag = _dot((scores * lmask).astype(BF16), xdt.astype(BF16))
            ys.append(y_diag + y_off[:, cols] * eac_c + dsk[:, cols] * xh)
            xds.append((xdt * ds_c).astype(BF16))
            eals.append(jnp.broadcast_to(eal_c, (l, p)))
        states = _dot_tn(bc, jnp.concatenate(xds, axis=1))
        h_scr[g] = h_in * jnp.concatenate(eals, axis=1) + states
        z = z_ref[rows, :]
        y = jnp.concatenate(ys, axis=1) * _silu(z)
        y = y * lax.rsqrt(jnp.mean(y * y, axis=-1, keepdims=True) + EPS) * nw
        o_ref[rows, :] = y.astype(o_ref.dtype)
        return carry

    lax.fori_loop(0, nch, body, 0)


def _ssd(proj, conv_w, conv_b, m_fields, m_rows, dskip, norm_w, batch, seq, col0_xbc, col0_z):
    bt = proj.shape[0]
    tbs = MIX_TB
    ntb = seq // tbs
    ng = SSM_GROUPS
    gw = SSM_R * SSM_P
    inner = SSM_HEADS * SSM_P
    xb0 = col0_xbc // gw
    bb0 = (col0_xbc + inner) // SSM_N
    cb0 = (col0_xbc + inner + ng * SSM_N) // SSM_N
    zb0 = col0_z // gw
    wb0 = inner // SSM_N
    wc0 = (inner + ng * SSM_N) // SSM_N
    row = lambda b, t, g: b * ntb + t
    return pl.pallas_call(
        _ssd_kernel,
        grid=(batch, ntb, ng),
        in_specs=[
            pl.BlockSpec((tbs, gw), lambda b, t, g: (row(b, t, g), xb0 + g)),
            pl.BlockSpec((tbs, SSM_N), lambda b, t, g: (row(b, t, g), bb0 + g)),
            pl.BlockSpec((tbs, SSM_N), lambda b, t, g: (row(b, t, g), cb0 + g)),
            pl.BlockSpec((tbs, gw), lambda b, t, g: (row(b, t, g), zb0 + g)),
            pl.BlockSpec((CONV_K, gw), lambda b, t, g: (0, g)),
            pl.BlockSpec((CONV_K, SSM_N), lambda b, t, g: (0, wb0 + g)),
            pl.BlockSpec((CONV_K, SSM_N), lambda b, t, g: (0, wc0 + g)),
            pl.BlockSpec((1, gw), lambda b, t, g: (0, g)),
            pl.BlockSpec((1, SSM_N), lambda b, t, g: (0, wb0 + g)),
            pl.BlockSpec((1, SSM_N), lambda b, t, g: (0, wc0 + g)),
            pl.BlockSpec((5, tbs, LANES), lambda b, t, g: (0, row(b, t, g), 0)),
            pl.BlockSpec((tbs // SSM_CHUNK, LANES, SSM_CHUNK), lambda b, t, g: (row(b, t, g), 0, 0)),
            pl.BlockSpec((1, gw), lambda b, t, g: (0, g)),
            pl.BlockSpec((1, gw), lambda b, t, g: (0, g)),
        ],
        out_specs=pl.BlockSpec((tbs, gw), lambda b, t, g: (row(b, t, g), g)),
        out_shape=jax.ShapeDtypeStruct((bt, inner), BF16),
        scratch_shapes=[
            pltpu.VMEM((tbs + 8, gw), F32),
            pltpu.VMEM((tbs + 8, SSM_N), F32),
            pltpu.VMEM((tbs + 8, SSM_N), F32),
            pltpu.VMEM((ng, 8, gw), F32),
            pltpu.VMEM((ng, 8, SSM_N), F32),
            pltpu.VMEM((ng, 8, SSM_N), F32),
            pltpu.VMEM((ng, SSM_N, gw), F32),
        ],
        compiler_params=pltpu.CompilerParams(
            dimension_semantics=("arbitrary", "arbitrary", "arbitrary"),
            vmem_limit_bytes=VMEM_LIMIT),
        name="ssd",
    )(proj, proj, proj, proj, conv_w, conv_w, conv_w, conv_b, conv_b, conv_b,
      m_fields, m_rows, dskip, norm_w)


def _merge_kernel(o_ref, y_ref, wg_ref, ws_ref, ga_ref, gb_ref, out_ref):
    a = _dot(o_ref[...], wg_ref[...])
    b = _dot(y_ref[...], ws_ref[...])
    out_ref[...] = (_sigmoid(ga_ref[...]) * a + _sigmoid(gb_ref[...]) * b).astype(out_ref.dtype)


def _merge(o, y, wg, ws, proj, col0_ga, col0_gb, tm=512, tn=1024):
    bt, d_in = o.shape
    d = wg.shape[1]
    ga0, gb0 = col0_ga // tn, col0_gb // tn
    return pl.pallas_call(
        _merge_kernel,
        grid=(d // tn, bt // tm),
        in_specs=[
            pl.BlockSpec((tm, d_in), lambda j, i: (i, 0)),
            pl.BlockSpec((tm, d_in), lambda j, i: (i, 0)),
            pl.BlockSpec((d_in, tn), lambda j, i: (0, j)),
            pl.BlockSpec((d_in, tn), lambda j, i: (0, j)),
            pl.BlockSpec((tm, tn), lambda j, i: (i, ga0 + j)),
            pl.BlockSpec((tm, tn), lambda j, i: (i, gb0 + j)),
        ],
        out_specs=pl.BlockSpec((tm, tn), lambda j, i: (i, j)),
        out_shape=jax.ShapeDtypeStruct((bt, d), BF16),
        compiler_params=pltpu.CompilerParams(
            dimension_semantics=("parallel", "parallel"),
            vmem_limit_bytes=VMEM_LIMIT),
        name="merge",
    )(o, y, wg, ws, proj, proj)


def _oproj_kernel(m_ref, w_ref, h_ref, out_ref):
    out_ref[...] = h_ref[...] + _dot(m_ref[...], w_ref[...])


def _oproj(merged, wo, h, tm=512, tn=1024):
    bt, d_in = merged.shape
    d = wo.shape[1]
    return pl.pallas_call(
        _oproj_kernel,
        grid=(d // tn, bt // tm),
        in_specs=[
            pl.BlockSpec((tm, d_in), lambda j, i: (i, 0)),
            pl.BlockSpec((d_in, tn), lambda j, i: (0, j)),
            pl.BlockSpec((tm, tn), lambda j, i: (i, j)),
        ],
        out_specs=pl.BlockSpec((tm, tn), lambda j, i: (i, j)),
        out_shape=jax.ShapeDtypeStruct((bt, d), F32),
        compiler_params=pltpu.CompilerParams(
            dimension_semantics=("parallel", "parallel"),
            vmem_limit_bytes=VMEM_LIMIT),
        name="oproj",
    )(merged, wo, h)


def _mlp_kernel(h_ref, nw_ref, wu_ref, wd_ref, fw_ref, out_ref, xn_scr, acc_scr, *, final):
    k = pl.program_id(1)

    @pl.when(k == 0)
    def _():
        x = h_ref[...]
        xn = x * lax.rsqrt(jnp.mean(x * x, axis=-1, keepdims=True) + EPS) * nw_ref[...]
        xn_scr[...] = xn.astype(BF16)
        acc_scr[...] = jnp.zeros_like(acc_scr)

    up = _dot(xn_scr[...], wu_ref[...])
    act = jnp.square(jnp.maximum(up, 0.0)).astype(BF16)
    acc_scr[...] += _dot(act, wd_ref[...])

    @pl.when(k == pl.num_programs(1) - 1)
    def _():
        y = h_ref[...] + acc_scr[...]
        if final:
            y = y * lax.rsqrt(jnp.mean(y * y, axis=-1, keepdims=True) + EPS) * fw_ref[...]
        out_ref[...] = y


def _mlp(h, norm_w, w_up, w_down, final_w, final, tm=512, th=1024):
    bt, d = h.shape
    hid = w_up.shape[1]
    return pl.pallas_call(
        functools.partial(_mlp_kernel, final=final),
        grid=(bt // tm, hid // th),
        in_specs=[
            pl.BlockSpec((tm, d), lambda i, k: (i, 0)),
            pl.BlockSpec((1, d), lambda i, k: (0, 0)),
            pl.BlockSpec((d, th), lambda i, k: (0, k)),
            pl.BlockSpec((th, d), lambda i, k: (k, 0)),
            pl.BlockSpec((1, d), lambda i, k: (0, 0)),
        ],
        out_specs=pl.BlockSpec((tm, d), lambda i, k: (i, 0)),
        out_shape=jax.ShapeDtypeStruct((bt, d), F32),
        scratch_shapes=[pltpu.VMEM((tm, d), BF16), pltpu.VMEM((tm, d), F32)],
        compiler_params=pltpu.CompilerParams(
            dimension_semantics=("parallel", "arbitrary"),
            vmem_limit_bytes=VMEM_LIMIT),
        name="mlp",
    )(h, norm_w, w_up, w_down, final_w)


def kernel(x, norm1_w, w_in, gdn_conv_w, gdn_a_log, gdn_dt_bias, gdn_norm_w, ssm_conv_w, ssm_conv_b, ssm_a_log, ssm_dt_bias, ssm_d, ssm_norm_w, w_gdn_out, w_ssm_out, w_o, norm2_w, w_up, w_down, final_norm_w):
    batch, seq, d = x.shape
    depth = w_in.shape[0]
    assert depth >= 1
    gw = GDN_HEADS * GDN_DK
    inner = SSM_HEADS * SSM_P
    conv_ch = inner + 2 * SSM_GROUPS * SSM_N
    sizes = (3 * gw, gw, GDN_HEADS, GDN_HEADS, inner, conv_ch, SSM_HEADS, d, d)
    offs = [0]
    for s in sizes:
        offs.append(offs[-1] + s)
    o_qkv, o_gz, o_gb, o_ga, o_sz, o_xbc, o_dt, o_gate_a, o_gate_b, _ = offs
    main_parts = ((o_qkv, 3 * gw), (o_gz, gw), (o_sz, inner), (o_xbc, conv_ch),
                  (o_gate_a, d), (o_gate_b, d))
    m_offs = [0]
    for _, s in main_parts:
        m_offs.append(m_offs[-1] + s)
    c_qkv, c_gz, c_sz, c_xbc, c_ga, c_gb, _ = m_offs

    h = x.reshape(batch * seq, d)
    for l in range(depth):
        w = w_in[l]
        w_main = jnp.concatenate([w[:, a:a + s] for a, s in main_parts], axis=1).astype(BF16)
        w_small = jnp.concatenate(
            [w[:, o_gb:o_gb + GDN_HEADS], w[:, o_ga:o_ga + GDN_HEADS], w[:, o_dt:o_dt + SSM_HEADS],
             jnp.zeros((d, LANES - 2 * GDN_HEADS - SSM_HEADS), w.dtype)], axis=1).astype(BF16)
        proj, small = _inproj(h, norm1_w[l][None, :], w_main, w_small)

        zpad = jnp.zeros((LANES - 2 * GDN_HEADS - SSM_HEADS,), F32)
        params = jnp.zeros((8, LANES), F32)
        params = params.at[0].set(jnp.concatenate(
            [jnp.zeros((GDN_HEADS,), F32), gdn_dt_bias[l], ssm_dt_bias[l], zpad]))
        params = params.at[1].set(jnp.concatenate(
            [jnp.zeros((GDN_HEADS,), F32), gdn_a_log[l], ssm_a_log[l], zpad]))
        g_fields, m_fields = _smallprep(small, params)
        bt = batch * seq
        g_rows = g_fields[0:3].reshape(3, bt // GDN_CHUNK, GDN_CHUNK, LANES).transpose(0, 1, 3, 2)
        m_rows = m_fields[1].reshape(bt // SSM_CHUNK, SSM_CHUNK, LANES).transpose(0, 2, 1)

        o = _gdn(proj, gdn_conv_w[l], g_fields, g_rows, gdn_norm_w[l][None, :],
                 batch, seq, c_qkv, c_gz)
        dskip = jnp.repeat(ssm_d[l], SSM_P)[None, :]
        y = _ssd(proj, ssm_conv_w[l], ssm_conv_b[l][None, :], m_fields, m_rows, dskip,
                 ssm_norm_w[l][None, :], batch, seq, c_xbc, c_sz)
        merged = _merge(o, y, w_gdn_out[l].astype(BF16), w_ssm_out[l].astype(BF16),
                        proj, c_ga, c_gb)
        h = _oproj(merged, w_o[l].astype(BF16), h)
        h = _mlp(h, norm2_w[l][None, :], w_up[l].astype(BF16), w_down[l].astype(BF16),
                 final_norm_w[None, :], final=(l == depth - 1))
    return h.reshape(batch, seq, d)
```

```python
import functools

import jax
import jax.numpy as jnp
from jax import lax
from jax.experimental import pallas as pl
from jax.experimental.pallas import tpu as pltpu

F32 = jnp.float32
BF16 = jnp.bfloat16
HIGHEST = lax.Precision.HIGHEST
EPS = 1e-6

LANES = 128
CONV_K = 4
GDN_HEADS = 16
GDN_DK = 128
GDN_CHUNK = 64
GDN_HB = 4
GDN_CPI = 2
SSM_HEADS = 32
SSM_P = 64
SSM_GROUPS = 8
SSM_N = 128
SSM_CHUNK = 128
SSM_R = SSM_HEADS // SSM_GROUPS
MIX_TB = 512
VMEM_LIMIT = 48 * 1024 * 1024

LANE_BETA = 0
LANE_GDEC = 16
LANE_DT = 32


def _sigmoid(x):
    return 1.0 / (1.0 + jnp.exp(-x))


def _silu(x):
    return x * _sigmoid(x)


def _dot(a, b, precision=None):
    return jnp.dot(a, b, preferred_element_type=F32, precision=precision)


def _dot3(a, b):
    ah = a.astype(BF16)
    al = (a - ah.astype(F32)).astype(BF16)
    bh = b.astype(BF16)
    bl = (b - bh.astype(F32)).astype(BF16)
    return _dot(ah, bh) + (_dot(ah, bl) + _dot(al, bh))


def _dot_nt(a, b):
    return lax.dot_general(a, b, (((1,), (1,)), ((), ())), preferred_element_type=F32)


def _dot_tn(a, b):
    return lax.dot_general(a, b, (((0,), (0,)), ((), ())), preferred_element_type=F32)


def _inproj_kernel(x_ref, nw_ref, wm_ref, ws_ref, pm_ref, ps_ref, xn_scr):
    @pl.when(pl.program_id(1) == 0)
    def _():
        x = x_ref[...]
        xn = x * lax.rsqrt(jnp.mean(x * x, axis=-1, keepdims=True) + EPS) * nw_ref[...]
        xn_scr[...] = xn.astype(BF16)
        ps_ref[...] = _dot(xn_scr[...], ws_ref[...])

    pm_ref[...] = _dot(xn_scr[...], wm_ref[...])


def _inproj(x2, norm_w, w_main, w_small, tm=1024, tn=1024):
    bt, d = x2.shape
    n = w_main.shape[1]
    return pl.pallas_call(
        _inproj_kernel,
        grid=(bt // tm, n // tn),
        in_specs=[
            pl.BlockSpec((tm, d), lambda i, j: (i, 0)),
            pl.BlockSpec((1, d), lambda i, j: (0, 0)),
            pl.BlockSpec((d, tn), lambda i, j: (0, j)),
            pl.BlockSpec((d, LANES), lambda i, j: (0, 0)),
        ],
        out_specs=[
            pl.BlockSpec((tm, tn), lambda i, j: (i, j)),
            pl.BlockSpec((tm, LANES), lambda i, j: (i, 0)),
        ],
        out_shape=[
            jax.ShapeDtypeStruct((bt, n), F32),
            jax.ShapeDtypeStruct((bt, LANES), F32),
        ],
        scratch_shapes=[pltpu.VMEM((tm, d), BF16)],
        compiler_params=pltpu.CompilerParams(
            dimension_semantics=("parallel", "arbitrary"),
            vmem_limit_bytes=VMEM_LIMIT),
        name="inproj",
    )(x2, norm_w, w_main, w_small)


def _smallprep_kernel(s_ref, p_ref, g_ref, m_ref):
    tm = s_ref.shape[0]
    bias = p_ref[0:1, :]
    neg_a = -jnp.exp(p_ref[1:2, :])
    ii = lax.broadcasted_iota(jnp.int32, (LANES, LANES), 0)
    jj = lax.broadcasted_iota(jnp.int32, (LANES, LANES), 1)
    same64 = (ii // GDN_CHUNK) == (jj // GDN_CHUNK)
    tril128 = jnp.where(ii >= jj, 1.0, 0.0).astype(F32)
    ones64 = jnp.where(same64, 1.0, 0.0).astype(F32)
    tril64 = tril128 * ones64
    ones128 = jnp.ones((LANES, LANES), F32)
    for s in range(tm // LANES):
        rows = slice(s * LANES, (s + 1) * LANES)
        x = s_ref[rows, :]
        xb = x + bias
        sp = jnp.maximum(xb, 0.0) + jnp.log1p(jnp.exp(-jnp.abs(xb)))
        dec = neg_a * sp
        c64 = _dot(tril64, dec, HIGHEST)
        l64 = _dot(ones64, dec, HIGHEST)
        c128 = _dot(tril128, dec, HIGHEST)
        l128 = _dot(ones128, dec, HIGHEST)
        g_ref[0, rows, :] = _sigmoid(x)
        g_ref[1, rows, :] = c64
        g_ref[2, rows, :] = jnp.exp(c64)
        g_ref[3, rows, :] = jnp.exp(l64 - c64)
        g_ref[4, rows, :] = jnp.exp(l64)
        m_ref[0, rows, :] = sp
        m_ref[1, rows, :] = c128
        m_ref[2, rows, :] = jnp.exp(c128)
        m_ref[3, rows, :] = jnp.exp(l128 - c128)
        m_ref[4, rows, :] = jnp.exp(l128)


def _smallprep(small, params, tm=512):
    bt = small.shape[0]
    spec = pl.BlockSpec((5, tm, LANES), lambda i: (0, i, 0))
    shp = jax.ShapeDtypeStruct((5, bt, LANES), F32)
    return pl.pallas_call(
        _smallprep_kernel,
        grid=(bt // tm,),
        in_specs=[pl.BlockSpec((tm, LANES), lambda i: (i, 0)),
                  pl.BlockSpec((8, LANES), lambda i: (0, 0))],
        out_specs=[spec, spec],
        out_shape=[shp, shp],
        compiler_params=pltpu.CompilerParams(dimension_semantics=("parallel",)),
        name="smallprep",
    )(small, params)


def _conv_silu(xe_scr, r0, rows, w_ref, b_ref=None):
    win = xe_scr[pl.ds(r0, rows + 8), :]
    acc = win[8:] * w_ref[CONV_K - 1:CONV_K, :]
    for k in range(CONV_K - 1):
        acc = acc + pltpu.roll(win, CONV_K - 1 - k, axis=0)[8:] * w_ref[k:k + 1, :]
    if b_ref is not None:
        acc = acc + b_ref[...]
    return _silu(acc)


def _col(field, lane_iota, lane):
    return jnp.sum(jnp.where(lane_iota == lane, field, 0.0), axis=-1, keepdims=True)


def _gdn_kernel(q_ref, k_ref, v_ref, z_ref, cwq_ref, cwk_ref, cwv_ref, sm_ref, smt_ref, nw_ref,
                o_ref,
                xq_scr, xk_scr, xv_scr, tail_scr, s_scr, qp_scr, op_scr, phi_scr, psi_scr, egl_scr):
    tb = pl.program_id(1)
    hg = pl.program_id(2)
    tbs = q_ref.shape[0]
    nch = tbs // GDN_CHUNK
    c = GDN_CHUNK
    dk = GDN_DK

    @pl.when(tb == 0)
    def _():
        tail_scr[:, hg] = jnp.zeros((3,) + tail_scr.shape[2:], F32)
        s_scr[pl.ds(hg * GDN_HB, GDN_HB)] = jnp.zeros((GDN_HB, dk, dk), F32)

    for idx, (src, scr) in enumerate(((q_ref, xq_scr), (k_ref, xk_scr), (v_ref, xv_scr))):
        scr[0:8, :] = tail_scr[idx, hg]
        scr[8:8 + tbs, :] = src[...]
        tail_scr[idx, hg] = src[tbs - 8:tbs, :]

    ii = lax.broadcasted_iota(jnp.int32, (c, c), 0)
    jj = lax.broadcasted_iota(jnp.int32, (c, c), 1)
    causal = ii >= jj
    strict = ii > jj
    blk16 = (ii // 16) == (jj // 16)
    blk32 = (ii // 32) == (jj // 32)
    eye = jnp.where(ii == jj, 1.0, 0.0).astype(F32)
    lane_iota = lax.broadcasted_iota(jnp.int32, (c, LANES), 1)
    p_half = lane_iota >= c
    qscale = dk ** -0.5
    cpi = GDN_CPI

    def phase_a(it, carry):
        chains = [(cc, r) for cc in range(cpi) for r in range(GDN_HB)]
        conv = []
        for cc in range(cpi):
            r0 = pl.multiple_of((it * cpi + cc) * c, c)
            conv.append((r0,
                         _conv_silu(xq_scr, r0, c, cwq_ref),
                         _conv_silu(xk_scr, r0, c, cwk_ref),
                         _conv_silu(xv_scr, r0, c, cwv_ref),
                         sm_ref[:, pl.ds(r0, c), :]))
        st = []
        for cc, r in chains:
            r0, qc, kc, vc, sm = conv[cc]
            ci = it * cpi + cc
            h = hg * GDN_HB + r
            cols = slice(r * dk, (r + 1) * dk)
            beta_c = _col(sm[0], lane_iota, LANE_BETA + h)
            gc_c = _col(sm[1], lane_iota, LANE_GDEC + h)
            egc_c = _col(sm[2], lane_iota, LANE_GDEC + h)
            ekl_c = _col(sm[3], lane_iota, LANE_GDEC + h)
            egl_c = _col(sm[4], lane_iota, LANE_GDEC + h)
            beta_r = smt_ref[0, ci, pl.ds(LANE_BETA + h, 1), :]
            gc_r = smt_ref[1, ci, pl.ds(LANE_GDEC + h, 1), :]
            egc_r = smt_ref[2, ci, pl.ds(LANE_GDEC + h, 1), :]
            q = qc[:, cols]
            k = kc[:, cols]
            qn = q * lax.rsqrt(jnp.sum(q * q, axis=-1, keepdims=True) + EPS) * qscale
            kn = k * lax.rsqrt(jnp.sum(k * k, axis=-1, keepdims=True) + EPS)
            kb = kn.astype(BF16)
            gram = _dot_nt(jnp.concatenate([kb, qn.astype(BF16)], axis=0), kb)
            dec = jnp.where(causal, jnp.exp(jnp.where(causal, gc_c - gc_r, 0.0)), 0.0)
            a_mat = jnp.where(strict, gram[:c] * beta_c * dec, 0.0)
            st.append(dict(
                rows=pl.ds(r0, c), ci=ci, r=r, a=a_mat, attn=(gram[c:] * dec).astype(BF16),
                kb=kb, vb=vc[:, cols].astype(BF16), beta_r=beta_r, bg_r=beta_r * egc_r,
                qg=qn * egc_c, kd=(kn * ekl_c).astype(BF16), egl_c=egl_c))

        dd = [jnp.where(blk16, s_["a"], 0.0) for s_ in st]
        wst = [jnp.concatenate([-d_, eye], axis=1) for d_ in dd]
        for _ in range(3):
            outs = [_dot3(w_[:, :c], w_) for w_ in wst]
            wst = [o_ + jnp.where(p_half, w_, 0.0) for o_, w_ in zip(outs, wst)]
        tinv = [w_[:, c:] + _dot3(w_[:, :c], w_[:, c:]) for w_ in wst]
        a32 = [jnp.where(blk32, s_["a"], 0.0) for s_ in st]
        for lo, hi in ((dd, a32), (a32, [s_["a"] for s_ in st])):
            tb_ = [t_.astype(BF16) for t_ in tinv]
            te = [_dot(t_, (h_ - l_).astype(BF16)).astype(BF16) for t_, h_, l_ in zip(tb_, hi, lo)]
            tinv = [t_ - _dot(e_, b_) for t_, e_, b_ in zip(tinv, te, tb_)]

        for s_, t_ in zip(st, tinv):
            r, rows, ci = s_["r"], s_["rows"], s_["ci"]
            u = _dot((t_ * s_["beta_r"]).astype(BF16), s_["vb"])
            w = _dot((t_ * s_["bg_r"]).astype(BF16), s_["kb"])
            wu = jnp.concatenate([w, u], axis=1).astype(BF16)
            aw = _dot(s_["attn"], wu)
            kw = _dot_tn(s_["kd"], wu)
            qp_scr[r, rows, :] = (s_["qg"] - aw[:, :dk]).astype(BF16)
            op_scr[r, rows, :] = aw[:, dk:]
            phi_scr[r, ci] = (-kw[:, :dk]).astype(BF16)
            psi_scr[r, ci] = kw[:, dk:]
            egl_scr[r, rows, :] = jnp.broadcast_to(s_["egl_c"], (c, dk))
        return carry

    lax.fori_loop(0, nch // cpi, phase_a, 0)

    nw = nw_ref[...]

    def phase_b(ci, carry):
        r0 = pl.multiple_of(ci * c, c)
        rows = pl.ds(r0, c)
        ss = [s_scr[hg * GDN_HB + r] for r in range(GDN_HB)]
        sb = [s_.astype(BF16) for s_ in ss]
        for r in range(GDN_HB):
            egl = egl_scr[r, rows, :]
            s_scr[hg * GDN_HB + r] = (ss[r] * jnp.concatenate([egl, egl], axis=0)
                                      + _dot(phi_scr[r, ci], sb[r]) + psi_scr[r, ci])
        for r in range(GDN_HB):
            cols = slice(r * dk, (r + 1) * dk)
            o = _dot(qp_scr[r, rows, :], sb[r]) + op_scr[r, rows, :]
            z = z_ref[rows, cols]
            on = o * lax.rsqrt(jnp.mean(o * o, axis=-1, keepdims=True) + EPS) * nw * _silu(z)
            o_ref[rows, cols] = on.astype(o_ref.dtype)
        return carry

    lax.fori_loop(0, nch, phase_b, 0)


def _gdn(proj, conv_w, g_fields, g_rows, norm_w, batch, seq, col0_qkv, col0_z):
    bt = proj.shape[0]
    tbs = MIX_TB
    ntb = seq // tbs
    nhg = GDN_HEADS // GDN_HB
    cw = GDN_HB * GDN_DK
    width = GDN_HEADS * GDN_DK
    qb, kb_, vb, zb = (col0_qkv // cw, (col0_qkv + width) // cw, (col0_qkv + 2 * width) // cw,
                       col0_z // cw)
    row = lambda b, t, h: b * ntb + t
    return pl.pallas_call(
        _gdn_kernel,
        grid=(batch, ntb, nhg),
        in_specs=[
            pl.BlockSpec((tbs, cw), lambda b, t, h: (row(b, t, h), qb + h)),
            pl.BlockSpec((tbs, cw), lambda b, t, h: (row(b, t, h), kb_ + h)),
            pl.BlockSpec((tbs, cw), lambda b, t, h: (row(b, t, h), vb + h)),
            pl.BlockSpec((tbs, cw), lambda b, t, h: (row(b, t, h), zb + h)),
            pl.BlockSpec((CONV_K, cw), lambda b, t, h: (0, h)),
            pl.BlockSpec((CONV_K, cw), lambda b, t, h: (0, nhg + h)),
            pl.BlockSpec((CONV_K, cw), lambda b, t, h: (0, 2 * nhg + h)),
            pl.BlockSpec((5, tbs, LANES), lambda b, t, h: (0, row(b, t, h), 0)),
            pl.BlockSpec((3, tbs // GDN_CHUNK, LANES, GDN_CHUNK),
                         lambda b, t, h: (0, row(b, t, h), 0, 0)),
            pl.BlockSpec((1, GDN_DK), lambda b, t, h: (0, 0)),
        ],
        out_specs=pl.BlockSpec((tbs, cw), lambda b, t, h: (row(b, t, h), h)),
        out_shape=jax.ShapeDtypeStruct((bt, width), BF16),
        scratch_shapes=[
            pltpu.VMEM((tbs + 8, cw), F32),
            pltpu.VMEM((tbs + 8, cw), F32),
            pltpu.VMEM((tbs + 8, cw), F32),
            pltpu.VMEM((3, nhg, 8, cw), F32),
            pltpu.VMEM((GDN_HEADS, GDN_DK, GDN_DK), F32),
            pltpu.VMEM((GDN_HB, tbs, GDN_DK), BF16),
            pltpu.VMEM((GDN_HB, tbs, GDN_DK), F32),
            pltpu.VMEM((GDN_HB, tbs // GDN_CHUNK, GDN_DK, GDN_DK), BF16),
            pltpu.VMEM((GDN_HB, tbs // GDN_CHUNK, GDN_DK, GDN_DK), F32),
            pltpu.VMEM((GDN_HB, tbs, GDN_DK), F32),
        ],
        compiler_params=pltpu.CompilerParams(
            dimension_semantics=("arbitrary", "arbitrary", "arbitrary"),
            vmem_limit_bytes=VMEM_LIMIT),
        name="gdn",
    )(proj, proj, proj, proj, conv_w, conv_w, conv_w, g_fields, g_rows, norm_w)


def _ssd_kernel(x_ref, b_ref, c_ref, z_ref, cwx_ref, cwb_ref, cwc_ref, cbx_ref, cbb_ref, cbc_ref,
                sm_ref, smt_ref, dsk_ref, nw_ref,
                o_ref,
                xx_scr, xb_scr, xc_scr, tailx_scr, tailb_scr, tailc_scr, h_scr):
    tb = pl.program_id(1)
    g = pl.program_id(2)
    tbs = x_ref.shape[0]
    l = SSM_CHUNK
    p = SSM_P
    nch = tbs // l

    @pl.when(tb == 0)
    def _():
        tailx_scr[g] = jnp.zeros(tailx_scr.shape[1:], F32)
        tailb_scr[g] = jnp.zeros(tailb_scr.shape[1:], F32)
        tailc_scr[g] = jnp.zeros(tailc_scr.shape[1:], F32)
        h_scr[g] = jnp.zeros(h_scr.shape[1:], F32)

    for src, scr, tail in ((x_ref, xx_scr, tailx_scr), (b_ref, xb_scr, tailb_scr),
                           (c_ref, xc_scr, tailc_scr)):
        scr[0:8, :] = tail[g]
        scr[8:8 + tbs, :] = src[...]
        tail[g] = src[tbs - 8:tbs, :]

    ii = lax.broadcasted_iota(jnp.int32, (l, l), 0)
    jj = lax.broadcasted_iota(jnp.int32, (l, l), 1)
    causal = ii >= jj
    lane_iota = lax.broadcasted_iota(jnp.int32, (l, LANES), 1)
    dsk = dsk_ref[...]
    nw = nw_ref[...]

    def body(ci, carry):
        r0 = pl.multiple_of(ci * l, l)
        rows = pl.ds(r0, l)
        xc = _conv_silu(xx_scr, r0, l, cwx_ref, cbx_ref)
        bc = _conv_silu(xb_scr, r0, l, cwb_ref, cbb_ref).astype(BF16)
        cc = _conv_silu(xc_scr, r0, l, cwc_ref, cbc_ref).astype(BF16)
        scores = _dot_nt(cc, bc)
        sm = sm_ref[:, rows, :]
        h_in = h_scr[g]
        y_off = _dot(cc, h_in.astype(BF16))
        ys, xds, eals = [], [], []
        for r in range(SSM_R):
            lane = LANE_DT + g * SSM_R + r
            cols = slice(r * p, (r + 1) * p)
            dt_c = _col(sm[0], lane_iota, lane)
            ac_c = _col(sm[1], lane_iota, lane)
            eac_c = _col(sm[2], lane_iota, lane)
            ds_c = _col(sm[3], lane_iota, lane)
            eal_c = _col(sm[4], lane_iota, lane)
            ac_r = smt_ref[ci, pl.ds(lane, 1), :]
            lmask = jnp.where(causal, jnp.exp(jnp.where(causal, ac_c - ac_r, 0.0)), 0.0)
            xh = xc[:, cols]
            xdt = xh * dt_c
            y_diag = _dot((scores * lmask).astype(BF16), xdt.astype(BF16))
            ys.append(y_diag + y_off[:, cols] * eac_c + dsk[:, cols] * xh)
            xds.append((xdt * ds_c).astype(BF16))
            eals.append(jnp.broadcast_to(eal_c, (l, p)))
        states = _dot_tn(bc, jnp.concatenate(xds, axis=1))
        h_scr[g] = h_in * jnp.concatenate(eals, axis=1) + states
        z = z_ref[rows, :]
        y = jnp.concatenate(ys, axis=1) * _silu(z)
        y = y * lax.rsqrt(jnp.mean(y * y, axis=-1, keepdims=True) + EPS) * nw
        o_ref[rows, :] = y.astype(o_ref.dtype)
        return carry

    lax.fori_loop(0, nch, body, 0)


def _ssd(proj, conv_w, conv_b, m_fields, m_rows, dskip, norm_w, batch, seq, col0_xbc, col0_z):
    bt = proj.shape[0]
    tbs = MIX_TB
    ntb = seq // tbs
    ng = SSM_GROUPS
    gw = SSM_R * SSM_P
    inner = SSM_HEADS * SSM_P
    xb0 = col0_xbc // gw
    bb0 = (col0_xbc + inner) // SSM_N
    cb0 = (col0_xbc + inner + ng * SSM_N) // SSM_N
    zb0 = col0_z // gw
    wb0 = inner // SSM_N
    wc0 = (inner + ng * SSM_N) // SSM_N
    row = lambda b, t, g: b * ntb + t
    return pl.pallas_call(
        _ssd_kernel,
        grid=(batch, ntb, ng),
        in_specs=[
            pl.BlockSpec((tbs, gw), lambda b, t, g: (row(b, t, g), xb0 + g)),
            pl.BlockSpec((tbs, SSM_N), lambda b, t, g: (row(b, t, g), bb0 + g)),
            pl.BlockSpec((tbs, SSM_N), lambda b, t, g: (row(b, t, g), cb0 + g)),
            pl.BlockSpec((tbs, gw), lambda b, t, g: (row(b, t, g), zb0 + g)),
            pl.BlockSpec((CONV_K, gw), lambda b, t, g: (0, g)),
            pl.BlockSpec((CONV_K, SSM_N), lambda b, t, g: (0, wb0 + g)),
            pl.BlockSpec((CONV_K, SSM_N), lambda b, t, g: (0, wc0 + g)),
            pl.BlockSpec((1, gw), lambda b, t, g: (0, g)),
            pl.BlockSpec((1, SSM_N), lambda b, t, g: (0, wb0 + g)),
            pl.BlockSpec((1, SSM_N), lambda b, t, g: (0, wc0 + g)),
            pl.BlockSpec((5, tbs, LANES), lambda b, t, g: (0, row(b, t, g), 0)),
            pl.BlockSpec((tbs // SSM_CHUNK, LANES, SSM_CHUNK), lambda b, t, g: (row(b, t, g), 0, 0)),
            pl.BlockSpec((1, gw), lambda b, t, g: (0, g)),
            pl.BlockSpec((1, gw), lambda b, t, g: (0, g)),
        ],
        out_specs=pl.BlockSpec((tbs, gw), lambda b, t, g: (row(b, t, g), g)),
        out_shape=jax.ShapeDtypeStruct((bt, inner), BF16),
        scratch_shapes=[
            pltpu.VMEM((tbs + 8, gw), F32),
            pltpu.VMEM((tbs + 8, SSM_N), F32),
            pltpu.VMEM((tbs + 8, SSM_N), F32),
            pltpu.VMEM((ng, 8, gw), F32),
            pltpu.VMEM((ng, 8, SSM_N), F32),
            pltpu.VMEM((ng, 8, SSM_N), F32),
            pltpu.VMEM((ng, SSM_N, gw), F32),
        ],
        compiler_params=pltpu.CompilerParams(
            dimension_semantics=("arbitrary", "arbitrary", "arbitrary"),
            vmem_limit_bytes=VMEM_LIMIT),
        name="ssd",
    )(proj, proj, proj, proj, conv_w, conv_w, conv_w, conv_b, conv_b, conv_b,
      m_fields, m_rows, dskip, norm_w)


def _merge_kernel(o_ref, y_ref, wg_ref, ws_ref, ga_ref, gb_ref, out_ref):
    a = _dot(o_ref[...], wg_ref[...])
    b = _dot(y_ref[...], ws_ref[...])
    out_ref[...] = (_sigmoid(ga_ref[...]) * a + _sigmoid(gb_ref[...]) * b).astype(out_ref.dtype)


def _merge(o, y, wg, ws, proj, col0_ga, col0_gb, tm=512, tn=1024):
    bt, d_in = o.shape
    d = wg.shape[1]
    ga0, gb0 = col0_ga // tn, col0_gb // tn
    return pl.pallas_call(
        _merge_kernel,
        grid=(d // tn, bt // tm),
        in_specs=[
            pl.BlockSpec((tm, d_in), lambda j, i: (i, 0)),
            pl.BlockSpec((tm, d_in), lambda j, i: (i, 0)),
            pl.BlockSpec((d_in, tn), lambda j, i: (0, j)),
            pl.BlockSpec((d_in, tn), lambda j, i: (0, j)),
            pl.BlockSpec((tm, tn), lambda j, i: (i, ga0 + j)),
            pl.BlockSpec((tm, tn), lambda j, i: (i, gb0 + j)),
        ],
        out_specs=pl.BlockSpec((tm, tn), lambda j, i: (i, j)),
        out_shape=jax.ShapeDtypeStruct((bt, d), BF16),
        compiler_params=pltpu.CompilerParams(
            dimension_semantics=("parallel", "parallel"),
            vmem_limit_bytes=VMEM_LIMIT),
        name="merge",
    )(o, y, wg, ws, proj, proj)


def _oproj_kernel(m_ref, w_ref, h_ref, out_ref):
    out_ref[...] = h_ref[...] + _dot(m_ref[...], w_ref[...])


def _oproj(merged, wo, h, tm=512, tn=1024):
    bt, d_in = merged.shape
    d = wo.shape[1]
    return pl.pallas_call(
        _oproj_kernel,
        grid=(d // tn, bt // tm),
        in_specs=[
            pl.BlockSpec((tm, d_in), lambda j, i: (i, 0)),
            pl.BlockSpec((d_in, tn), lambda j, i: (0, j)),
            pl.BlockSpec((tm, tn), lambda j, i: (i, j)),
        ],
        out_specs=pl.BlockSpec((tm, tn), lambda j, i: (i, j)),
        out_shape=jax.ShapeDtypeStruct((bt, d), F32),
        compiler_params=pltpu.CompilerParams(
            dimension_semantics=("parallel", "parallel"),
            vmem_limit_bytes=VMEM_LIMIT),
        name="oproj",
    )(merged, wo, h)


def _mlp_kernel(h_ref, nw_ref, wu_ref, wd_ref, fw_ref, out_ref, xn_scr, acc_scr, *, final):
    k = pl.program_id(1)

    @pl.when(k == 0)
    def _():
        x = h_ref[...]
        xn = x * lax.rsqrt(jnp.mean(x * x, axis=-1, keepdims=True) + EPS) * nw_ref[...]
        xn_scr[...] = xn.astype(BF16)
        acc_scr[...] = jnp.zeros_like(acc_scr)

    up = _dot(xn_scr[...], wu_ref[...])
    act = jnp.square(jnp.maximum(up, 0.0)).astype(BF16)
    acc_scr[...] += _dot(act, wd_ref[...])

    @pl.when(k == pl.num_programs(1) - 1)
    def _():
        y = h_ref[...] + acc_scr[...]
        if final:
            y = y * lax.rsqrt(jnp.mean(y * y, axis=-1, keepdims=True) + EPS) * fw_ref[...]
        out_ref[...] = y


def _mlp(h, norm_w, w_up, w_down, final_w, final, tm=512, th=1024):
    bt, d = h.shape
    hid = w_up.shape[1]
    return pl.pallas_call(
        functools.partial(_mlp_kernel, final=final),
        grid=(bt // tm, hid // th),
        in_specs=[
            pl.BlockSpec((tm, d), lambda i, k: (i, 0)),
            pl.BlockSpec((1, d), lambda i, k: (0, 0)),
            pl.BlockSpec((d, th), lambda i, k: (0, k)),
            pl.BlockSpec((th, d), lambda i, k: (k, 0)),
            pl.BlockSpec((1, d), lambda i, k: (0, 0)),
        ],
        out_specs=pl.BlockSpec((tm, d), lambda i, k: (i, 0)),
        out_shape=jax.ShapeDtypeStruct((bt, d), F32),
        scratch_shapes=[pltpu.VMEM((tm, d), BF16), pltpu.VMEM((tm, d), F32)],
        compiler_params=pltpu.CompilerParams(
            dimension_semantics=("parallel", "arbitrary"),
            vmem_limit_bytes=VMEM_LIMIT),
        name="mlp",
    )(h, norm_w, w_up, w_down, final_w)


def kernel(x, norm1_w, w_in, gdn_conv_w, gdn_a_log, gdn_dt_bias, gdn_norm_w, ssm_conv_w, ssm_conv_b, ssm_a_log, ssm_dt_bias, ssm_d, ssm_norm_w, w_gdn_out, w_ssm_out, w_o, norm2_w, w_up, w_down, final_norm_w):
    batch, seq, d = x.shape
    depth = w_in.shape[0]
    assert depth >= 1
    gw = GDN_HEADS * GDN_DK
    inner = SSM_HEADS * SSM_P
    conv_ch = inner + 2 * SSM_GROUPS * SSM_N
    sizes = (3 * gw, gw, GDN_HEADS, GDN_HEADS, inner, conv_ch, SSM_HEADS, d, d)
    offs = [0]
    for s in sizes:
        offs.append(offs[-1] + s)
    o_qkv, o_gz, o_gb, o_ga, o_sz, o_xbc, o_dt, o_gate_a, o_gate_b, _ = offs
    main_parts = ((o_qkv, 3 * gw), (o_gz, gw), (o_sz, inner), (o_xbc, conv_ch),
                  (o_gate_a, d), (o_gate_b, d))
    m_offs = [0]
    for _, s in main_parts:
        m_offs.append(m_offs[-1] + s)
    c_qkv, c_gz, c_sz, c_xbc, c_ga, c_gb, _ = m_offs

    h = x.reshape(batch * seq, d)
    for l in range(depth):
        w = w_in[l]
        w_main = jnp.concatenate([w[:, a:a + s] for a, s in main_parts], axis=1).astype(BF16)
        w_small = jnp.concatenate(
            [w[:, o_gb:o_gb + GDN_HEADS], w[:, o_ga:o_ga + GDN_HEADS], w[:, o_dt:o_dt + SSM_HEADS],
             jnp.zeros((d, LANES - 2 * GDN_HEADS - SSM_HEADS), w.dtype)], axis=1).astype(BF16)
        proj, small = _inproj(h, norm1_w[l][None, :], w_main, w_small)

        zpad = jnp.zeros((LANES - 2 * GDN_HEADS - SSM_HEADS,), F32)
        params = jnp.zeros((8, LANES), F32)
        params = params.at[0].set(jnp.concatenate(
            [jnp.zeros((GDN_HEADS,), F32), gdn_dt_bias[l], ssm_dt_bias[l], zpad]))
        params = params.at[1].set(jnp.concatenate(
            [jnp.zeros((GDN_HEADS,), F32), gdn_a_log[l], ssm_a_log[l], zpad]))
        g_fields, m_fields = _smallprep(small, params)
        bt = batch * seq
        g_rows = g_fields[0:3].reshape(3, bt // GDN_CHUNK, GDN_CHUNK, LANES).transpose(0, 1, 3, 2)
        m_rows = m_fields[1].reshape(bt // SSM_CHUNK, SSM_CHUNK, LANES).transpose(0, 2, 1)

        o = _gdn(proj, gdn_conv_w[l], g_fields, g_rows, gdn_norm_w[l][None, :],
                 batch, seq, c_qkv, c_gz)
        dskip = jnp.repeat(ssm_d[l], SSM_P)[None, :]
        y = _ssd(proj, ssm_conv_w[l], ssm_conv_b[l][None, :], m_fields, m_rows, dskip,
                 ssm_norm_w[l][None, :], batch, seq, c_xbc, c_sz)
        merged = _merge(o, y, w_gdn_out[l].astype(BF16), w_ssm_out[l].astype(BF16),
                        proj, c_ga, c_gb)
        h = _oproj(merged, w_o[l].astype(BF16), h)
        h = _mlp(h, norm2_w[l][None, :], w_up[l].astype(BF16), w_down[l].astype(BF16),
                 final_norm_w[None, :], final=(l == depth - 1))
    return h.reshape(batch, seq, d)
```

```python
import functools

import jax
import jax.numpy as jnp
from jax import lax
from jax.experimental import pallas as pl
from jax.experimental.pallas import tpu as pltpu

F32 = jnp.float32
BF16 = jnp.bfloat16
HIGHEST = lax.Precision.HIGHEST
EPS = 1e-6

LANES = 128
CONV_K = 4
GDN_HEADS = 16
GDN_DK = 128
GDN_CHUNK = 128
GDN_HB = 4
GDN_CPI = 2
SSM_HEADS = 32
SSM_P = 64
SSM_GROUPS = 8
SSM_N = 128
SSM_CHUNK = 128
SSM_R = SSM_HEADS // SSM_GROUPS
SSD_CPI = 2
MIX_TB = 512
PROJ_SUB = 256
PROJ_ROWS = 128
VMEM_LIMIT = 48 * 1024 * 1024

LANE_BETA = 0
LANE_GDEC = 16
LANE_DT = 32


def _sigmoid(x):
    return 1.0 / (1.0 + jnp.exp(-x))


def _silu(x):
    return x * _sigmoid(x)


def _dot(a, b, precision=None):
    return jnp.dot(a, b, preferred_element_type=F32, precision=precision)


def _dot3(a, b):
    ah = a.astype(BF16)
    al = (a - ah.astype(F32)).astype(BF16)
    bh = b.astype(BF16)
    bl = (b - bh.astype(F32)).astype(BF16)
    return _dot(ah, bh) + (_dot(ah, bl) + _dot(al, bh))


def _dot_nt(a, b):
    return lax.dot_general(a, b, (((1,), (1,)), ((), ())), preferred_element_type=F32)


def _dot_tn(a, b):
    return lax.dot_general(a, b, (((0,), (0,)), ((), ())), preferred_element_type=F32)


def _rms(x):
    return x * lax.rsqrt(jnp.mean(x * x, axis=-1, keepdims=True) + EPS)


def _norm_kernel(x_ref, w_ref, o_ref):
    o_ref[...] = (_rms(x_ref[...]) * w_ref[...]).astype(o_ref.dtype)


def _norm(x2, w, tm=512):
    bt, d = x2.shape
    return pl.pallas_call(
        _norm_kernel,
        grid=(bt // tm,),
        in_specs=[pl.BlockSpec((tm, d), lambda i: (i, 0)), pl.BlockSpec((1, d), lambda i: (0, 0))],
        out_specs=pl.BlockSpec((tm, d), lambda i: (i, 0)),
        out_shape=jax.ShapeDtypeStruct((bt, d), BF16),
        compiler_params=pltpu.CompilerParams(dimension_semantics=("parallel",)),
        name="norm",
    )(x2, w)


def _proj_kernel(xn_ref, w_ref, cw_ref, cb_ref, o_ref, acc_scr, *, conv, act, l2norm, q_tiles, seq):
    j = pl.program_id(0)
    i = pl.program_id(1)
    tm, tn = o_ref.shape
    nsub = tn // PROJ_SUB
    if conv:
        @pl.when((i * tm) % seq == 0)
        def _():
            for s in range(nsub):
                acc_scr[s, 0:8, :] = jnp.zeros((8, PROJ_SUB), F32)
    if l2norm:
        scale = jnp.where(j < q_tiles, GDN_DK ** -0.5, 1.0)

    def matmul(s):
        acc_scr[s, 8:, :] = _dot(xn_ref[...], w_ref[:, s * PROJ_SUB:(s + 1) * PROJ_SUB])

    def epilogue(s):
        cols = slice(s * PROJ_SUB, (s + 1) * PROJ_SUB)
        for rc in range(tm // PROJ_ROWS):
            r0 = rc * PROJ_ROWS
            if conv:
                win = acc_scr[s, r0:r0 + PROJ_ROWS + 8, :]
                prev = pltpu.roll(win, 1, axis=0)
                lo = win * cw_ref[1:2, cols] + prev * cw_ref[0:1, cols]
                y = (win * cw_ref[3:4, cols] + prev * cw_ref[2:3, cols]
                     + pltpu.roll(lo, 2, axis=0))[8:] + cb_ref[:, cols]
            else:
                y = acc_scr[s, 8 + r0:8 + r0 + PROJ_ROWS, :]
            y = _silu(y) if act == "silu" else _sigmoid(y)
            if l2norm:
                parts = []
                for hh in range(PROJ_SUB // GDN_DK):
                    yh = y[:, hh * GDN_DK:(hh + 1) * GDN_DK]
                    parts.append(
                        yh * (lax.rsqrt(jnp.sum(yh * yh, axis=-1, keepdims=True) + EPS) * scale))
                y = jnp.concatenate(parts, axis=1)
            o_ref[r0:r0 + PROJ_ROWS, cols] = y.astype(o_ref.dtype)
        if conv:
            acc_scr[s, 0:8, :] = acc_scr[s, tm:tm + 8, :]

    matmul(0)
    for s in range(nsub):
        if s + 1 < nsub:
            matmul(s + 1)
        epilogue(s)


def _proj(xn, w, cw, cb, *, conv, act, l2norm, q_tiles, seq, name, tm=1024, tn=1024):
    bt, d = xn.shape
    n = w.shape[1]
    assert seq % tm == 0 and n % tn == 0
    body = functools.partial(_proj_kernel, conv=conv, act=act, l2norm=l2norm, q_tiles=q_tiles, seq=seq)
    return pl.pallas_call(
        body,
        grid=(n // tn, bt // tm),
        in_specs=[
            pl.BlockSpec((tm, d), lambda j, i: (i, 0)),
            pl.BlockSpec((d, tn), lambda j, i: (0, j)),
            pl.BlockSpec((CONV_K, tn), lambda j, i: (0, j)),
            pl.BlockSpec((1, tn), lambda j, i: (0, j)),
        ],
        out_specs=pl.BlockSpec((tm, tn), lambda j, i: (i, j)),
        out_shape=jax.ShapeDtypeStruct((bt, n), BF16),
        scratch_shapes=[pltpu.VMEM((tn // PROJ_SUB, tm + 8, PROJ_SUB), F32)],
        compiler_params=pltpu.CompilerParams(
            dimension_semantics=("arbitrary", "arbitrary"),
            vmem_limit_bytes=VMEM_LIMIT),
        name=name,
    )(xn, w, cw, cb)


def _smallprep_kernel(xn_ref, w_ref, p_ref, gc_ref, gr_ref, mc_ref, mr_ref):
    tm = xn_ref.shape[0]
    bias = p_ref[0:1, :]
    neg_a = -jnp.exp(p_ref[1:2, :])
    ii = lax.broadcasted_iota(jnp.int32, (LANES, LANES), 0)
    jj = lax.broadcasted_iota(jnp.int32, (LANES, LANES), 1)
    tril = jnp.where(ii >= jj, 1.0, 0.0).astype(F32)
    ones = jnp.ones((LANES, LANES), F32)
    logits = _dot(xn_ref[...], w_ref[...])
    for s in range(tm // LANES):
        rows = slice(s * LANES, (s + 1) * LANES)
        x = logits[rows, :]
        xb = x + bias
        sp = jnp.maximum(xb, 0.0) + jnp.log1p(jnp.exp(-jnp.abs(xb)))
        dec = neg_a * sp
        cum = _dot(tril, dec, HIGHEST)
        last = _dot(ones, dec, HIGHEST)
        beta = _sigmoid(x)
        ecum = jnp.exp(cum)
        etail = jnp.exp(last - cum)
        elast = jnp.exp(last)
        for f, val in enumerate((beta, cum, ecum, etail)):
            gc_ref[f, rows, :] = val
        for f, val in enumerate((beta, cum, ecum, elast)):
            gr_ref[f, s] = val.T
        for f, val in enumerate((cum, ecum)):
            mc_ref[f, rows, :] = val
        for f, val in enumerate((sp, cum, sp * etail, elast)):
            mr_ref[f, s] = val.T


def _smallprep(xn, w_small, params, tm=512):
    bt, d = xn.shape
    nb = tm // LANES

    def cspec(nf):
        return pl.BlockSpec((nf, tm, LANES), lambda i: (0, i, 0))

    def rspec(nf):
        return pl.BlockSpec((nf, nb, LANES, LANES), lambda i: (0, i, 0, 0))

    def cshape(nf):
        return jax.ShapeDtypeStruct((nf, bt, LANES), F32)

    def rshape(nf):
        return jax.ShapeDtypeStruct((nf, bt // LANES, LANES, LANES), F32)

    return pl.pallas_call(
        _smallprep_kernel,
        grid=(bt // tm,),
        in_specs=[pl.BlockSpec((tm, d), lambda i: (i, 0)),
                  pl.BlockSpec((d, LANES), lambda i: (0, 0)),
                  pl.BlockSpec((8, LANES), lambda i: (0, 0))],
        out_specs=[cspec(4), rspec(4), cspec(2), rspec(4)],
        out_shape=[cshape(4), rshape(4), cshape(2), rshape(4)],
        compiler_params=pltpu.CompilerParams(dimension_semantics=("parallel",)),
        name="smallprep",
    )(xn, w_small, params)


def _col(field, lane_iota, lane):
    return jnp.sum(jnp.where(lane_iota == lane, field, 0.0), axis=-1, keepdims=True)


def _gdn_kernel(q_ref, k_ref, v_ref, z_ref, sm_ref, smt_ref, nw_ref,
                o_ref,
                s_scr, qp_scr, op_scr, phi_scr, psi_scr, egl_scr):
    tb = pl.program_id(1)
    hg = pl.program_id(2)
    tbs = q_ref.shape[0]
    nch = tbs // GDN_CHUNK
    c = GDN_CHUNK
    dk = GDN_DK

    @pl.when(tb == 0)
    def _():
        s_scr[pl.ds(hg * GDN_HB, GDN_HB)] = jnp.zeros((GDN_HB, dk, dk), F32)

    ii = lax.broadcasted_iota(jnp.int32, (c, c), 0)
    jj = lax.broadcasted_iota(jnp.int32, (c, c), 1)
    causal = ii >= jj
    strict = ii > jj
    nlev = 4
    blk = {w: (ii // w) == (jj // w) for w in (16, 32, 64, 128) if w < c}
    eye = jnp.where(ii == jj, 1.0, 0.0).astype(F32)
    lane_iota = lax.broadcasted_iota(jnp.int32, (c, LANES), 1)
    p_half = lax.broadcasted_iota(jnp.int32, (c, 2 * c), 1) >= c
    cpi = GDN_CPI

    def phase_a(it, carry):
        st = []
        for cc in range(cpi):
            ci = it * cpi + cc
            rows = pl.ds(pl.multiple_of(ci * c, c), c)
            sm = sm_ref[:, rows, :]
            for r in range(GDN_HB):
                h = hg * GDN_HB + r
                cols = slice(r * dk, (r + 1) * dk)
                beta_r = smt_ref[0, ci, pl.ds(LANE_BETA + h, 1), :]
                gc_r, egc_r, egl_r = (
                    smt_ref[f, ci, pl.ds(LANE_GDEC + h, 1), :] for f in (1, 2, 3))
                st.append(dict(
                    rows=rows, ci=ci, r=r, h=h, sm=sm, qb=q_ref[rows, cols], kb=k_ref[rows, cols],
                    vb=v_ref[rows, cols], beta_r=beta_r, gc_r=gc_r, bg_r=beta_r * egc_r,
                    egl=jnp.broadcast_to(egl_r, (8, dk))))
        grams = [_dot_nt(jnp.concatenate([s_["kb"], s_["qb"]], axis=0), s_["kb"]) for s_ in st]
        for f, name, lane0 in ((0, "beta_c", LANE_BETA), (1, "gc_c", LANE_GDEC),
                               (2, "egc_c", LANE_GDEC), (3, "ekl_c", LANE_GDEC)):
            for s_ in st:
                s_[name] = _col(s_["sm"][f], lane_iota, lane0 + s_["h"])
        decs = [jnp.where(causal, jnp.exp(jnp.where(causal, s_["gc_c"] - s_["gc_r"], 0.0)), 0.0)
                for s_ in st]
        for s_, gram, dec in zip(st, grams, decs):
            s_["a"] = jnp.where(strict, gram[:c] * s_["beta_c"] * dec, 0.0)
            s_["attn"] = (gram[c:] * dec).astype(BF16)
        for s_ in st:
            s_["qg"] = s_["qb"].astype(F32) * s_["egc_c"]
            s_["kd"] = (s_["kb"].astype(F32) * s_["ekl_c"]).astype(BF16)

        widths = sorted(blk)
        parts = [[jnp.where(blk[w], s_["a"], 0.0) for s_ in st] for w in widths]
        parts.append([s_["a"] for s_ in st])
        wst = [jnp.concatenate([-d_, eye], axis=1) for d_ in parts[0]]
        for _ in range(nlev - 1):
            outs = [_dot3(w_[:, :c], w_) for w_ in wst]
            wst = [o_ + jnp.where(p_half, w_, 0.0) for o_, w_ in zip(outs, wst)]
        tinv = [w_[:, c:] + _dot3(w_[:, :c], w_[:, c:]) for w_ in wst]
        for lo, hi in zip(parts[:-1], parts[1:]):
            tb_ = [t_.astype(BF16) for t_ in tinv]
            te = [_dot(t_, (h_ - l_).astype(BF16)).astype(BF16) for t_, h_, l_ in zip(tb_, hi, lo)]
            tinv = [t_ - _dot(e_, b_) for t_, e_, b_ in zip(tinv, te, tb_)]

        us = [_dot((t_ * s_["beta_r"]).astype(BF16), s_["vb"]) for s_, t_ in zip(st, tinv)]
        ws = [_dot((t_ * s_["bg_r"]).astype(BF16), s_["kb"]) for s_, t_ in zip(st, tinv)]
        wus = [jnp.concatenate([w_, u_], axis=1).astype(BF16) for w_, u_ in zip(ws, us)]
        aws = [_dot(s_["attn"], wu_) for s_, wu_ in zip(st, wus)]
        kws = [_dot_tn(s_["kd"], wu_) for s_, wu_ in zip(st, wus)]
        for s_, aw, kw in zip(st, aws, kws):
            r, rows, ci = s_["r"], s_["rows"], s_["ci"]
            qp_scr[r, rows, :] = (s_["qg"] - aw[:, :dk]).astype(BF16)
            op_scr[r, rows, :] = aw[:, dk:]
            phi_scr[r, ci] = (-kw[:, :dk]).astype(BF16)
            psi_scr[r, ci] = kw[:, dk:]
            egl_scr[r, ci] = s_["egl"]
        return carry

    lax.fori_loop(0, nch // cpi, phase_a, 0)

    nw = nw_ref[...]

    def phase_b(ci, carry):
        r0 = pl.multiple_of(ci * c, c)
        rows = pl.ds(r0, c)
        heads = range(GDN_HB)
        ss = [s_scr[hg * GDN_HB + r] for r in heads]
        sb = [s_.astype(BF16) for s_ in ss]
        ds = [_dot(phi_scr[r, ci], sb[r]) for r in heads]
        for r in heads:
            s_scr[hg * GDN_HB + r] = ss[r] * egl_scr[r, ci, 0:1, :] + ds[r] + psi_scr[r, ci]
        os_ = [_dot(qp_scr[r, rows, :], sb[r]) + op_scr[r, rows, :] for r in heads]
        ons = [_rms(o) * nw for o in os_]
        for r in heads:
            cols = slice(r * dk, (r + 1) * dk)
            o_ref[rows, cols] = (ons[r] * z_ref[rows, cols].astype(F32)).astype(o_ref.dtype)
        return carry

    lax.fori_loop(0, nch, phase_b, 0)


def _gdn(qk, v, z, g_fields, g_rows, norm_w, batch, seq):
    bt = qk.shape[0]
    tbs = MIX_TB
    ntb = seq // tbs
    nhg = GDN_HEADS // GDN_HB
    cw = GDN_HB * GDN_DK
    width = GDN_HEADS * GDN_DK
    row = lambda b, t, h: b * ntb + t
    return pl.pallas_call(
        _gdn_kernel,
        grid=(batch, ntb, nhg),
        in_specs=[
            pl.BlockSpec((tbs, cw), lambda b, t, h: (row(b, t, h), h)),
            pl.BlockSpec((tbs, cw), lambda b, t, h: (row(b, t, h), nhg + h)),
            pl.BlockSpec((tbs, cw), lambda b, t, h: (row(b, t, h), h)),
            pl.BlockSpec((tbs, cw), lambda b, t, h: (row(b, t, h), h)),
            pl.BlockSpec((4, tbs, LANES), lambda b, t, h: (0, row(b, t, h), 0)),
            pl.BlockSpec((4, tbs // LANES, LANES, LANES), lambda b, t, h: (0, row(b, t, h), 0, 0)),
            pl.BlockSpec((1, GDN_DK), lambda b, t, h: (0, 0)),
        ],
        out_specs=pl.BlockSpec((tbs, cw), lambda b, t, h: (row(b, t, h), h)),
        out_shape=jax.ShapeDtypeStruct((bt, width), BF16),
        scratch_shapes=[
            pltpu.VMEM((GDN_HEADS, GDN_DK, GDN_DK), F32),
            pltpu.VMEM((GDN_HB, tbs, GDN_DK), BF16),
            pltpu.VMEM((GDN_HB, tbs, GDN_DK), F32),
            pltpu.VMEM((GDN_HB, tbs // GDN_CHUNK, GDN_DK, GDN_DK), BF16),
            pltpu.VMEM((GDN_HB, tbs // GDN_CHUNK, GDN_DK, GDN_DK), F32),
            pltpu.VMEM((GDN_HB, tbs // GDN_CHUNK, 8, GDN_DK), F32),
        ],
        compiler_params=pltpu.CompilerParams(
            dimension_semantics=("arbitrary", "arbitrary", "arbitrary"),
            vmem_limit_bytes=VMEM_LIMIT),
        name="gdn",
    )(qk, qk, v, z, g_fields, g_rows, norm_w)


def _ssd_kernel(x_ref, b_ref, c_ref, z_ref, sm_ref, smt_ref, dsk_ref, nw_ref,
                o_ref,
                h_scr):
    tb = pl.program_id(1)
    g = pl.program_id(2)
    tbs = x_ref.shape[0]
    l = SSM_CHUNK
    p = SSM_P
    nch = tbs // l

    @pl.when(tb == 0)
    def _():
        h_scr[g] = jnp.zeros(h_scr.shape[1:], F32)

    ii = lax.broadcasted_iota(jnp.int32, (l, l), 0)
    jj = lax.broadcasted_iota(jnp.int32, (l, l), 1)
    causal = ii >= jj
    lane_iota = lax.broadcasted_iota(jnp.int32, (l, LANES), 1)
    dsk = dsk_ref[...]
    nw = nw_ref[...]

    first_of_pair = lane_iota < p
    npair = SSM_R // 2
    cpi = SSD_CPI

    def pick(a0, a1):
        return jnp.where(first_of_pair[:a0.shape[0]], a0, a1)

    def body(it, carry):
        h = h_scr[g]
        st = []
        for cc_ in range(cpi):
            ci = it * cpi + cc_
            rows = pl.ds(pl.multiple_of(ci * l, l), l)
            bc = b_ref[rows, :]
            cm = c_ref[rows, :]
            sm = sm_ref[:, rows, :]
            rowf = [[smt_ref[f, ci, pl.ds(LANE_DT + g * SSM_R + r, 1), :] for f in range(4)]
                    for r in range(SSM_R)]
            st.append(dict(rows=rows, xb=x_ref[rows, :], cm=cm, sm=sm, rowf=rowf,
                           scores=_dot_nt(cm, bc), bct=bc.astype(F32).T))
        for s_ in st:
            s_["ac_c"] = [_col(s_["sm"][0], lane_iota, LANE_DT + g * SSM_R + r) for r in range(SSM_R)]
            s_["eac_c"] = [_col(s_["sm"][1], lane_iota, LANE_DT + g * SSM_R + r) for r in range(SSM_R)]
        for s_ in st:
            s_["m"] = [(s_["scores"] * jnp.where(
                causal, jnp.exp(jnp.where(causal, s_["ac_c"][r] - s_["rowf"][r][1], 0.0)), 0.0)
                * s_["rowf"][r][0]).astype(BF16) for r in range(SSM_R)]
            s_["bt"] = [(s_["bct"] * s_["rowf"][r][2]).astype(BF16) for r in range(SSM_R)]
        for s_ in st:
            yd, stt, eac, eal = [], [], [], []
            for pr in range(npair):
                xp = s_["xb"][:, pr * LANES:(pr + 1) * LANES]
                r0_, r1_ = 2 * pr, 2 * pr + 1
                yd.append(pick(_dot(s_["m"][r0_], xp), _dot(s_["m"][r1_], xp)))
                stt.append(pick(_dot(s_["bt"][r0_], xp), _dot(s_["bt"][r1_], xp)))
                eac.append(pick(jnp.broadcast_to(s_["eac_c"][r0_], (l, LANES)),
                                jnp.broadcast_to(s_["eac_c"][r1_], (l, LANES))))
                eal.append(pick(s_["rowf"][r0_][3], s_["rowf"][r1_][3]))
            s_["yd"] = jnp.concatenate(yd, axis=1)
            s_["states"] = jnp.concatenate(stt, axis=1)
            s_["eac"] = jnp.concatenate(eac, axis=1)
            s_["eal"] = jnp.concatenate(eal, axis=1)
        for s_ in st:
            y_off = _dot(s_["cm"], h.astype(BF16))
            h = h * s_["eal"] + s_["states"]
            y = (s_["yd"] + y_off * s_["eac"] + dsk * s_["xb"].astype(F32)) \
                * z_ref[s_["rows"], :].astype(F32)
            o_ref[s_["rows"], :] = (_rms(y) * nw).astype(o_ref.dtype)
        h_scr[g] = h
        return carry

    lax.fori_loop(0, nch // cpi, body, 0)


def _ssd(xbc, z, m_fields, m_rows, dskip, norm_w, batch, seq, z_col0):
    bt = xbc.shape[0]
    tbs = MIX_TB
    ntb = seq // tbs
    ng = SSM_GROUPS
    gw = SSM_R * SSM_P
    inner = SSM_HEADS * SSM_P
    bb0 = inner // SSM_N
    cb0 = (inner + ng * SSM_N) // SSM_N
    zb0 = z_col0 // gw
    row = lambda b, t, g: b * ntb + t
    return pl.pallas_call(
        _ssd_kernel,
        grid=(batch, ntb, ng),
        in_specs=[
            pl.BlockSpec((tbs, gw), lambda b, t, g: (row(b, t, g), g)),
            pl.BlockSpec((tbs, SSM_N), lambda b, t, g: (row(b, t, g), bb0 + g)),
            pl.BlockSpec((tbs, SSM_N), lambda b, t, g: (row(b, t, g), cb0 + g)),
            pl.BlockSpec((tbs, gw), lambda b, t, g: (row(b, t, g), zb0 + g)),
            pl.BlockSpec((2, tbs, LANES), lambda b, t, g: (0, row(b, t, g), 0)),
            pl.BlockSpec((4, tbs // SSM_CHUNK, LANES, SSM_CHUNK),
                         lambda b, t, g: (0, row(b, t, g), 0, 0)),
            pl.BlockSpec((1, gw), lambda b, t, g: (0, g)),
            pl.BlockSpec((1, gw), lambda b, t, g: (0, g)),
        ],
        out_specs=pl.BlockSpec((tbs, gw), lambda b, t, g: (row(b, t, g), g)),
        out_shape=jax.ShapeDtypeStruct((bt, inner), BF16),
        scratch_shapes=[pltpu.VMEM((ng, SSM_N, gw), F32)],
        compiler_params=pltpu.CompilerParams(
            dimension_semantics=("arbitrary", "arbitrary", "arbitrary"),
            vmem_limit_bytes=VMEM_LIMIT),
        name="ssd",
    )(xbc, xbc, xbc, z, m_fields, m_rows, dskip, norm_w)


def _merge_kernel(o_ref, y_ref, wg_ref, ws_ref, ga_ref, gb_ref, out_ref):
    a = _dot(o_ref[...], wg_ref[...])
    b = _dot(y_ref[...], ws_ref[...])
    out_ref[...] = (ga_ref[...].astype(F32) * a + gb_ref[...].astype(F32) * b).astype(out_ref.dtype)


def _merge(o, y, wg, ws, ga, gb, tm=512, tn=1024):
    bt, d_in = o.shape
    d = wg.shape[1]
    return pl.pallas_call(
        _merge_kernel,
        grid=(d // tn, bt // tm),
        in_specs=[
            pl.BlockSpec((tm, d_in), lambda j, i: (i, 0)),
            pl.BlockSpec((tm, d_in), lambda j, i: (i, 0)),
            pl.BlockSpec((d_in, tn), lambda j, i: (0, j)),
            pl.BlockSpec((d_in, tn), lambda j, i: (0, j)),
            pl.BlockSpec((tm, tn), lambda j, i: (i, j)),
            pl.BlockSpec((tm, tn), lambda j, i: (i, j)),
        ],
        out_specs=pl.BlockSpec((tm, tn), lambda j, i: (i, j)),
        out_shape=jax.ShapeDtypeStruct((bt, d), BF16),
        compiler_params=pltpu.CompilerParams(
            dimension_semantics=("parallel", "parallel"),
            vmem_limit_bytes=VMEM_LIMIT),
        name="merge",
    )(o, y, wg, ws, ga, gb)


def _oproj_kernel(m_ref, w_ref, h_ref, out_ref):
    out_ref[...] = h_ref[...] + _dot(m_ref[...], w_ref[...])


def _oproj(merged, wo, h, tm=512, tn=1024):
    bt, d_in = merged.shape
    d = wo.shape[1]
    return pl.pallas_call(
        _oproj_kernel,
        grid=(d // tn, bt // tm),
        in_specs=[
            pl.BlockSpec((tm, d_in), lambda j, i: (i, 0)),
            pl.BlockSpec((d_in, tn), lambda j, i: (0, j)),
            pl.BlockSpec((tm, tn), lambda j, i: (i, j)),
        ],
        out_specs=pl.BlockSpec((tm, tn), lambda j, i: (i, j)),
        out_shape=jax.ShapeDtypeStruct((bt, d), F32),
        compiler_params=pltpu.CompilerParams(
            dimension_semantics=("parallel", "parallel"),
            vmem_limit_bytes=VMEM_LIMIT),
        name="oproj",
    )(merged, wo, h)


def _mlp_kernel(h_ref, nw_ref, wu_ref, wd_ref, fw_ref, out_ref, xn_scr, acc_scr, *, final):
    k = pl.program_id(1)

    @pl.when(k == 0)
    def _():
        xn_scr[...] = (_rms(h_ref[...]) * nw_ref[...]).astype(BF16)
        acc_scr[...] = jnp.zeros_like(acc_scr)

    up = _dot(xn_scr[...], wu_ref[...])
    act = jnp.square(jnp.maximum(up, 0.0)).astype(BF16)
    acc_scr[...] += _dot(act, wd_ref[...])

    @pl.when(k == pl.num_programs(1) - 1)
    def _():
        y = h_ref[...] + acc_scr[...]
        if final:
            y = _rms(y) * fw_ref[...]
        out_ref[...] = y


def _mlp(h, norm_w, w_up, w_down, final_w, final, tm=512, th=1024):
    bt, d = h.shape
    hid = w_up.shape[1]
    return pl.pallas_call(
        functools.partial(_mlp_kernel, final=final),
        grid=(bt // tm, hid // th),
        in_specs=[
            pl.BlockSpec((tm, d), lambda i, k: (i, 0)),
            pl.BlockSpec((1, d), lambda i, k: (0, 0)),
            pl.BlockSpec((d, th), lambda i, k: (0, k)),
            pl.BlockSpec((th, d), lambda i, k: (k, 0)),
            pl.BlockSpec((1, d), lambda i, k: (0, 0)),
        ],
        out_specs=pl.BlockSpec((tm, d), lambda i, k: (i, 0)),
        out_shape=jax.ShapeDtypeStruct((bt, d), F32),
        scratch_shapes=[pltpu.VMEM((tm, d), BF16), pltpu.VMEM((tm, d), F32)],
        compiler_params=pltpu.CompilerParams(
            dimension_semantics=("parallel", "arbitrary"),
            vmem_limit_bytes=VMEM_LIMIT),
        name="mlp",
    )(h, norm_w, w_up, w_down, final_w)


def kernel(x, norm1_w, w_in, gdn_conv_w, gdn_a_log, gdn_dt_bias, gdn_norm_w, ssm_conv_w, ssm_conv_b, ssm_a_log, ssm_dt_bias, ssm_d, ssm_norm_w, w_gdn_out, w_ssm_out, w_o, norm2_w, w_up, w_down, final_norm_w):
    batch, seq, d = x.shape
    depth = w_in.shape[0]
    assert depth >= 1
    bt = batch * seq
    gw = GDN_HEADS * GDN_DK
    inner = SSM_HEADS * SSM_P
    conv_ch = inner + 2 * SSM_GROUPS * SSM_N
    sizes = (3 * gw, gw, GDN_HEADS, GDN_HEADS, inner, conv_ch, SSM_HEADS, d, d)
    offs = [0]
    for s in sizes:
        offs.append(offs[-1] + s)
    o_qkv, o_gz, o_gb, o_ga, o_sz, o_xbc, o_dt, o_gate_a, o_gate_b, _ = offs
    no_w = jnp.zeros((CONV_K, max(gw, inner, d)), F32)
    no_b = jnp.zeros((1, max(2 * gw, conv_ch, d)), F32)
    zpad = jnp.zeros((LANES - 2 * GDN_HEADS - SSM_HEADS,), F32)

    h = x.reshape(bt, d)
    for l in range(depth):
        w = w_in[l]

        def seg(a, n):
            return w[:, a:a + n].astype(BF16)

        xn = _norm(h, norm1_w[l][None, :])
        proj = functools.partial(_proj, xn, seq=seq)
        qk = proj(seg(o_qkv, 2 * gw), gdn_conv_w[l][:, :2 * gw], no_b, conv=True, act="silu",
                  l2norm=True, q_tiles=gw // 1024, name="proj_qk")
        v = proj(seg(o_qkv + 2 * gw, gw), gdn_conv_w[l][:, 2 * gw:], no_b, conv=True, act="silu",
                 l2norm=False, q_tiles=0, name="proj_v")
        xbc = proj(seg(o_xbc, conv_ch), ssm_conv_w[l], ssm_conv_b[l][None, :], conv=True, act="silu",
                   l2norm=False, q_tiles=0, name="proj_xbc")
        gz = proj(seg(o_gz, gw), no_w, no_b, conv=False, act="silu", l2norm=False, q_tiles=0,
                  name="proj_gz")
        sz = proj(seg(o_sz, inner), no_w, no_b, conv=False, act="silu", l2norm=False, q_tiles=0,
                  name="proj_sz")
        ga = proj(seg(o_gate_a, d), no_w, no_b, conv=False, act="sigmoid", l2norm=False, q_tiles=0,
                  name="proj_ga")
        gb = proj(seg(o_gate_b, d), no_w, no_b, conv=False, act="sigmoid", l2norm=False, q_tiles=0,
                  name="proj_gb")

        w_small = jnp.concatenate(
            [w[:, o_gb:o_gb + GDN_HEADS], w[:, o_ga:o_ga + GDN_HEADS], w[:, o_dt:o_dt + SSM_HEADS],
             jnp.zeros((d, LANES - 2 * GDN_HEADS - SSM_HEADS), w.dtype)], axis=1).astype(BF16)
        params = jnp.zeros((8, LANES), F32)
        params = params.at[0].set(jnp.concatenate(
            [jnp.zeros((GDN_HEADS,), F32), gdn_dt_bias[l], ssm_dt_bias[l], zpad]))
        params = params.at[1].set(jnp.concatenate(
            [jnp.zeros((GDN_HEADS,), F32), gdn_a_log[l], ssm_a_log[l], zpad]))
        g_cols, g_rows, m_cols, m_rows = _smallprep(xn, w_small, params)

        o = _gdn(qk, v, gz, g_cols, g_rows, gdn_norm_w[l][None, :], batch, seq)
        dskip = jnp.repeat(ssm_d[l], SSM_P)[None, :]
        y = _ssd(xbc, sz, m_cols, m_rows, dskip, ssm_norm_w[l][None, :], batch, seq, 0)
        merged = _merge(o, y, w_gdn_out[l].astype(BF16), w_ssm_out[l].astype(BF16), ga, gb)
        h = _oproj(merged, w_o[l].astype(BF16), h)
        h = _mlp(h, norm2_w[l][None, :], w_up[l].astype(BF16), w_down[l].astype(BF16),
                 final_norm_w[None, :], final=(l == depth - 1))
    return h.reshape(batch, seq, d)
```

```python
import functools

import jax
import jax.numpy as jnp
from jax import lax
from jax.experimental import pallas as pl
from jax.experimental.pallas import tpu as pltpu

F32 = jnp.float32
BF16 = jnp.bfloat16
HIGHEST = lax.Precision.HIGHEST
EPS = 1e-6

LANES = 128
CONV_K = 4
GDN_HEADS = 16
GDN_DK = 128
GDN_CHUNK = 128
GDN_HB = 4
GDN_CPI = 4
SSM_HEADS = 32
SSM_P = 64
SSM_GROUPS = 8
SSM_N = 128
SSM_CHUNK = 128
SSM_R = SSM_HEADS // SSM_GROUPS
SSD_CPI = 2
MIX_TB = 512
PROJ_SUB = 256
PROJ_ROWS = 128
VMEM_LIMIT = 48 * 1024 * 1024

LANE_BETA = 0
LANE_GDEC = 16
LANE_DT = 32


def _sigmoid(x):
    return 1.0 / (1.0 + jnp.exp(-x))


def _silu(x):
    return x * _sigmoid(x)


def _dot(a, b, precision=None):
    return jnp.dot(a, b, preferred_element_type=F32, precision=precision)


def _dot3(a, b):
    ah = a.astype(BF16)
    al = (a - ah.astype(F32)).astype(BF16)
    bh = b.astype(BF16)
    bl = (b - bh.astype(F32)).astype(BF16)
    return _dot(ah, bh) + (_dot(ah, bl) + _dot(al, bh))


def _dot_nt(a, b):
    return lax.dot_general(a, b, (((1,), (1,)), ((), ())), preferred_element_type=F32)


def _dot_tn(a, b):
    return lax.dot_general(a, b, (((0,), (0,)), ((), ())), preferred_element_type=F32)


def _rms(x):
    return x * lax.rsqrt(jnp.mean(x * x, axis=-1, keepdims=True) + EPS)


def _norm_kernel(x_ref, w_ref, o_ref):
    o_ref[...] = (_rms(x_ref[...]) * w_ref[...]).astype(o_ref.dtype)


def _norm(x2, w, tm=512):
    bt, d = x2.shape
    return pl.pallas_call(
        _norm_kernel,
        grid=(bt // tm,),
        in_specs=[pl.BlockSpec((tm, d), lambda i: (i, 0)), pl.BlockSpec((1, d), lambda i: (0, 0))],
        out_specs=pl.BlockSpec((tm, d), lambda i: (i, 0)),
        out_shape=jax.ShapeDtypeStruct((bt, d), BF16),
        compiler_params=pltpu.CompilerParams(dimension_semantics=("parallel",)),
        name="norm",
    )(x2, w)


def _wprep_kernel(w_ref, *o_refs, segs, smalls):
    for (a, n), o_ref in zip(segs, o_refs[:-1]):
        o_ref[...] = w_ref[:, a:a + n].astype(o_ref.dtype)
    parts = [w_ref[:, a:a + n] for a, n in smalls]
    parts.append(jnp.zeros((w_ref.shape[0], LANES - sum(n for _, n in smalls)), F32))
    o_refs[-1][...] = jnp.concatenate(parts, axis=1).astype(o_refs[-1].dtype)


def _wprep(w, segs, smalls, tk=64):
    d, n_all = w.shape
    widths = [n for _, n in segs] + [LANES]
    return pl.pallas_call(
        functools.partial(_wprep_kernel, segs=segs, smalls=smalls),
        grid=(d // tk,),
        in_specs=[pl.BlockSpec((tk, n_all), lambda i: (i, 0))],
        out_specs=[pl.BlockSpec((tk, n), lambda i: (i, 0)) for n in widths],
        out_shape=[jax.ShapeDtypeStruct((d, n), BF16) for n in widths],
        compiler_params=pltpu.CompilerParams(dimension_semantics=("parallel",)),
        name="wprep",
    )(w)


def _proj_kernel(xn_ref, w_ref, cw_ref, cb_ref, o_ref, acc_scr, *, conv, act, l2norm, q_tiles, seq):
    j = pl.program_id(0)
    i = pl.program_id(1)
    tm, tn = o_ref.shape
    nsub = tn // PROJ_SUB
    if conv:
        @pl.when((i * tm) % seq == 0)
        def _():
            for s in range(nsub):
                acc_scr[s, 0:8, :] = jnp.zeros((8, PROJ_SUB), F32)
    if l2norm:
        scale = jnp.where(j < q_tiles, GDN_DK ** -0.5, 1.0)

    def matmul(s):
        acc_scr[s, 8:, :] = _dot(xn_ref[...], w_ref[:, s * PROJ_SUB:(s + 1) * PROJ_SUB])

    def epilogue(s):
        cols = slice(s * PROJ_SUB, (s + 1) * PROJ_SUB)
        for rc in range(tm // PROJ_ROWS):
            r0 = rc * PROJ_ROWS
            if conv:
                win = acc_scr[s, r0:r0 + PROJ_ROWS + 8, :]
                prev = pltpu.roll(win, 1, axis=0)
                lo = win * cw_ref[1:2, cols] + prev * cw_ref[0:1, cols]
                y = (win * cw_ref[3:4, cols] + prev * cw_ref[2:3, cols]
                     + pltpu.roll(lo, 2, axis=0))[8:] + cb_ref[:, cols]
            else:
                y = acc_scr[s, 8 + r0:8 + r0 + PROJ_ROWS, :]
            y = _silu(y) if act == "silu" else _sigmoid(y)
            if l2norm:
                parts = []
                for hh in range(PROJ_SUB // GDN_DK):
                    yh = y[:, hh * GDN_DK:(hh + 1) * GDN_DK]
                    parts.append(
                        yh * (lax.rsqrt(jnp.sum(yh * yh, axis=-1, keepdims=True) + EPS) * scale))
                y = jnp.concatenate(parts, axis=1)
            o_ref[r0:r0 + PROJ_ROWS, cols] = y.astype(o_ref.dtype)
        if conv:
            acc_scr[s, 0:8, :] = acc_scr[s, tm:tm + 8, :]

    matmul(0)
    for s in range(nsub):
        if s + 1 < nsub:
            matmul(s + 1)
        epilogue(s)


def _proj(xn, w, cw, cb, *, conv, act, l2norm, q_tiles, seq, name, tm=1024, tn=1024):
    bt, d = xn.shape
    n = w.shape[1]
    assert seq % tm == 0 and n % tn == 0
    body = functools.partial(_proj_kernel, conv=conv, act=act, l2norm=l2norm, q_tiles=q_tiles, seq=seq)
    return pl.pallas_call(
        body,
        grid=(n // tn, bt // tm),
        in_specs=[
            pl.BlockSpec((tm, d), lambda j, i: (i, 0)),
            pl.BlockSpec((d, tn), lambda j, i: (0, j)),
            pl.BlockSpec((CONV_K, tn), lambda j, i: (0, j)),
            pl.BlockSpec((1, tn), lambda j, i: (0, j)),
        ],
        out_specs=pl.BlockSpec((tm, tn), lambda j, i: (i, j)),
        out_shape=jax.ShapeDtypeStruct((bt, n), BF16),
        scratch_shapes=[pltpu.VMEM((tn // PROJ_SUB, tm + 8, PROJ_SUB), F32)],
        compiler_params=pltpu.CompilerParams(
            dimension_semantics=("arbitrary", "arbitrary"),
            vmem_limit_bytes=VMEM_LIMIT),
        name=name,
    )(xn, w, cw, cb)


def _smallprep_kernel(xn_ref, w_ref, p_ref, gc_ref, gr_ref, mc_ref, mr_ref):
    tm = xn_ref.shape[0]
    bias = p_ref[0:1, :]
    neg_a = -jnp.exp(p_ref[1:2, :])
    ii = lax.broadcasted_iota(jnp.int32, (LANES, LANES), 0)
    jj = lax.broadcasted_iota(jnp.int32, (LANES, LANES), 1)
    tril = jnp.where(ii >= jj, 1.0, 0.0).astype(F32)
    ones = jnp.ones((LANES, LANES), F32)
    logits = _dot(xn_ref[...], w_ref[...])
    for s in range(tm // LANES):
        rows = slice(s * LANES, (s + 1) * LANES)
        x = logits[rows, :]
        xb = x + bias
        sp = jnp.maximum(xb, 0.0) + jnp.log1p(jnp.exp(-jnp.abs(xb)))
        dec = neg_a * sp
        cum = _dot(tril, dec, HIGHEST)
        last = _dot(ones, dec, HIGHEST)
        beta = _sigmoid(x)
        ecum = jnp.exp(cum)
        etail = jnp.exp(last - cum)
        elast = jnp.exp(last)
        for f, val in enumerate((beta, cum, ecum, etail)):
            gc_ref[f, rows, :] = val
        for f, val in enumerate((beta, cum, ecum, elast)):
            gr_ref[f, s] = val.T
        for f, val in enumerate((cum, ecum)):
            mc_ref[f, rows, :] = val
        for f, val in enumerate((sp, cum, sp * etail, elast)):
            mr_ref[f, s] = val.T


def _smallprep(xn, w_small, params, tm=512):
    bt, d = xn.shape
    nb = tm // LANES

    def cspec(nf):
        return pl.BlockSpec((nf, tm, LANES), lambda i: (0, i, 0))

    def rspec(nf):
        return pl.BlockSpec((nf, nb, LANES, LANES), lambda i: (0, i, 0, 0))

    def cshape(nf):
        return jax.ShapeDtypeStruct((nf, bt, LANES), F32)

    def rshape(nf):
        return jax.ShapeDtypeStruct((nf, bt // LANES, LANES, LANES), F32)

    return pl.pallas_call(
        _smallprep_kernel,
        grid=(bt // tm,),
        in_specs=[pl.BlockSpec((tm, d), lambda i: (i, 0)),
                  pl.BlockSpec((d, LANES), lambda i: (0, 0)),
                  pl.BlockSpec((8, LANES), lambda i: (0, 0))],
        out_specs=[cspec(4), rspec(4), cspec(2), rspec(4)],
        out_shape=[cshape(4), rshape(4), cshape(2), rshape(4)],
        compiler_params=pltpu.CompilerParams(dimension_semantics=("parallel",)),
        name="smallprep",
    )(xn, w_small, params)


def _col(field, lane_iota, lane):
    return jnp.sum(jnp.where(lane_iota == lane, field, 0.0), axis=-1, keepdims=True)


def _gdn_kernel(q_ref, k_ref, v_ref, z_ref, sm_ref, smt_ref, nw_ref,
                o_ref,
                s_scr, qp_scr, op_scr, phi_scr, psi_scr, egl_scr):
    tb = pl.program_id(1)
    hg = pl.program_id(2)
    tbs = q_ref.shape[0]
    nch = tbs // GDN_CHUNK
    c = GDN_CHUNK
    dk = GDN_DK

    @pl.when(tb == 0)
    def _():
        s_scr[pl.ds(hg * GDN_HB, GDN_HB)] = jnp.zeros((GDN_HB, dk, dk), F32)

    ii = lax.broadcasted_iota(jnp.int32, (c, c), 0)
    jj = lax.broadcasted_iota(jnp.int32, (c, c), 1)
    causal = ii >= jj
    strict = ii > jj
    nlev = 4
    blk = {w: (ii // w) == (jj // w) for w in (16, 32, 64, 128) if w < c}
    lane_iota = lax.broadcasted_iota(jnp.int32, (c, LANES), 1)
    q_lane = lax.broadcasted_iota(jnp.int32, (16, c), 1)
    q_blk = q_lane // 16
    q_eye = jnp.where(q_lane % 16 == lax.broadcasted_iota(jnp.int32, (16, c), 0), 1.0, 0.0).astype(F32)
    cpi = GDN_CPI

    def phase_a(it, carry):
        st = []
        for cc in range(cpi):
            ci = it * cpi + cc
            rows = pl.ds(pl.multiple_of(ci * c, c), c)
            sm = sm_ref[:, rows, :]
            for r in range(GDN_HB):
                h = hg * GDN_HB + r
                cols = slice(r * dk, (r + 1) * dk)
                beta_r = smt_ref[0, ci, pl.ds(LANE_BETA + h, 1), :]
                gc_r, egc_r, egl_r = (
                    smt_ref[f, ci, pl.ds(LANE_GDEC + h, 1), :] for f in (1, 2, 3))
                st.append(dict(
                    rows=rows, ci=ci, r=r, h=h, sm=sm, qb=q_ref[rows, cols], kb=k_ref[rows, cols],
                    vb=v_ref[rows, cols], beta_r=beta_r, gc_r=gc_r, bg_r=beta_r * egc_r,
                    egl=jnp.broadcast_to(egl_r, (8, dk))))
        grams = [_dot_nt(jnp.concatenate([s_["kb"], s_["qb"]], axis=0), s_["kb"]) for s_ in st]
        for f, name, lane0 in ((0, "beta_c", LANE_BETA), (1, "gc_c", LANE_GDEC),
                               (2, "egc_c", LANE_GDEC), (3, "ekl_c", LANE_GDEC)):
            for s_ in st:
                s_[name] = _col(s_["sm"][f], lane_iota, lane0 + s_["h"])
        decs = [jnp.where(causal, jnp.exp(jnp.where(causal, s_["gc_c"] - s_["gc_r"], 0.0)), 0.0)
                for s_ in st]
        for s_, gram, dec in zip(st, grams, decs):
            s_["a"] = jnp.where(strict, gram[:c] * s_["beta_c"] * dec, 0.0)
            s_["attn"] = (gram[c:] * dec).astype(BF16)
        for s_ in st:
            s_["qg"] = s_["qb"].astype(F32) * s_["egc_c"]
            s_["kd"] = (s_["kb"].astype(F32) * s_["ekl_c"]).astype(BF16)

        widths = sorted(blk)
        parts = [[jnp.where(blk[w], s_["a"], 0.0) for s_ in st] for w in widths]
        parts.append([s_["a"] for s_ in st])

        def expand(q):
            return jnp.where(blk[16], jnp.concatenate([q] * (c // 16), axis=0), 0.0)

        qm, qp = [], []
        for s_ in st:
            acc = None
            for b in range(c // 16):
                term = jnp.where(q_blk == b, s_["a"][16 * b:16 * (b + 1), :], 0.0)
                acc = term if acc is None else acc + term
            qm.append(-acc)
            qp.append(q_eye)
        for lev in range(nlev):
            wm = [expand(m_) for m_ in qm]
            if lev + 1 < nlev:
                outs = [_dot3(jnp.concatenate([m_, p_], axis=0), w_) for m_, p_, w_ in zip(qm, qp, wm)]
                qm = [o_[:16] for o_ in outs]
                qp = [p_ + o_[16:] for p_, o_ in zip(qp, outs)]
            else:
                qp = [p_ + _dot3(p_, w_) for p_, w_ in zip(qp, wm)]
        tinv = [expand(p_) for p_ in qp]
        for lo, hi in zip(parts[:-1], parts[1:]):
            tb_ = [t_.astype(BF16) for t_ in tinv]
            te = [_dot(t_, (h_ - l_).astype(BF16)).astype(BF16) for t_, h_, l_ in zip(tb_, hi, lo)]
            tinv = [t_ - _dot(e_, b_) for t_, e_, b_ in zip(tinv, te, tb_)]

        us = [_dot((t_ * s_["beta_r"]).astype(BF16), s_["vb"]) for s_, t_ in zip(st, tinv)]
        ws = [_dot((t_ * s_["bg_r"]).astype(BF16), s_["kb"]) for s_, t_ in zip(st, tinv)]
        wus = [jnp.concatenate([w_, u_], axis=1).astype(BF16) for w_, u_ in zip(ws, us)]
        aws = [_dot(s_["attn"], wu_) for s_, wu_ in zip(st, wus)]
        kws = [_dot_tn(s_["kd"], wu_) for s_, wu_ in zip(st, wus)]
        for s_, aw, kw in zip(st, aws, kws):
            r, rows, ci = s_["r"], s_["rows"], s_["ci"]
            qp_scr[r, rows, :] = (s_["qg"] - aw[:, :dk]).astype(BF16)
            op_scr[r, rows, :] = aw[:, dk:]
            phi_scr[r, ci] = (-kw[:, :dk]).astype(BF16)
            psi_scr[r, ci] = kw[:, dk:]
            egl_scr[r, ci] = s_["egl"]
        return carry

    lax.fori_loop(0, nch // cpi, phase_a, 0)

    nw = nw_ref[...]

    heads = range(GDN_HB)
    ss = [s_scr[hg * GDN_HB + r] for r in heads]
    for ci in range(nch):
        rows = slice(ci * c, (ci + 1) * c)
        sb = [s_.astype(BF16) for s_ in ss]
        ds = [_dot(phi_scr[r, ci], sb[r]) for r in heads]
        ss = [ss[r] * egl_scr[r, ci, 0:1, :] + ds[r] + psi_scr[r, ci] for r in heads]
        os_ = [_dot(qp_scr[r, rows, :], sb[r]) + op_scr[r, rows, :] for r in heads]
        ons = [_rms(o) * nw for o in os_]
        for r in heads:
            cols = slice(r * dk, (r + 1) * dk)
            o_ref[rows, cols] = (ons[r] * z_ref[rows, cols].astype(F32)).astype(o_ref.dtype)
    for r in heads:
        s_scr[hg * GDN_HB + r] = ss[r]


def _gdn(qk, v, z, g_fields, g_rows, norm_w, batch, seq):
    bt = qk.shape[0]
    tbs = MIX_TB
    ntb = seq // tbs
    nhg = GDN_HEADS // GDN_HB
    cw = GDN_HB * GDN_DK
    width = GDN_HEADS * GDN_DK
    row = lambda b, t, h: b * ntb + t
    return pl.pallas_call(
        _gdn_kernel,
        grid=(batch, ntb, nhg),
        in_specs=[
            pl.BlockSpec((tbs, cw), lambda b, t, h: (row(b, t, h), h)),
            pl.BlockSpec((tbs, cw), lambda b, t, h: (row(b, t, h), nhg + h)),
            pl.BlockSpec((tbs, cw), lambda b, t, h: (row(b, t, h), h)),
            pl.BlockSpec((tbs, cw), lambda b, t, h: (row(b, t, h), h)),
            pl.BlockSpec((4, tbs, LANES), lambda b, t, h: (0, row(b, t, h), 0)),
            pl.BlockSpec((4, tbs // LANES, LANES, LANES), lambda b, t, h: (0, row(b, t, h), 0, 0)),
            pl.BlockSpec((1, GDN_DK), lambda b, t, h: (0, 0)),
        ],
        out_specs=pl.BlockSpec((tbs, cw), lambda b, t, h: (row(b, t, h), h)),
        out_shape=jax.ShapeDtypeStruct((bt, width), BF16),
        scratch_shapes=[
            pltpu.VMEM((GDN_HEADS, GDN_DK, GDN_DK), F32),
            pltpu.VMEM((GDN_HB, tbs, GDN_DK), BF16),
            pltpu.VMEM((GDN_HB, tbs, GDN_DK), F32),
            pltpu.VMEM((GDN_HB, tbs // GDN_CHUNK, GDN_DK, GDN_DK), BF16),
            pltpu.VMEM((GDN_HB, tbs // GDN_CHUNK, GDN_DK, GDN_DK), F32),
            pltpu.VMEM((GDN_HB, tbs // GDN_CHUNK, 8, GDN_DK), F32),
        ],
        compiler_params=pltpu.CompilerParams(
            dimension_semantics=("arbitrary", "arbitrary", "arbitrary"),
            vmem_limit_bytes=VMEM_LIMIT),
        name="gdn",
    )(qk, qk, v, z, g_fields, g_rows, norm_w)


def _ssd_kernel(x_ref, b_ref, c_ref, z_ref, sm_ref, smt_ref, dsk_ref, nw_ref,
                o_ref,
                h_scr):
    tb = pl.program_id(1)
    g = pl.program_id(2)
    tbs = x_ref.shape[0]
    l = SSM_CHUNK
    p = SSM_P
    nch = tbs // l

    @pl.when(tb == 0)
    def _():
        h_scr[g] = jnp.zeros(h_scr.shape[1:], F32)

    ii = lax.broadcasted_iota(jnp.int32, (l, l), 0)
    jj = lax.broadcasted_iota(jnp.int32, (l, l), 1)
    causal = ii >= jj
    lane_iota = lax.broadcasted_iota(jnp.int32, (l, LANES), 1)
    dsk = dsk_ref[...]
    nw = nw_ref[...]

    first_of_pair = lane_iota < p
    npair = SSM_R // 2
    cpi = SSD_CPI

    def pick(a0, a1):
        return jnp.where(first_of_pair[:a0.shape[0]], a0, a1)

    def body(it, carry):
        h = h_scr[g]
        st = []
        for cc_ in range(cpi):
            ci = it * cpi + cc_
            rows = pl.ds(pl.multiple_of(ci * l, l), l)
            bc = b_ref[rows, :]
            cm = c_ref[rows, :]
            sm = sm_ref[:, rows, :]
            rowf = [[smt_ref[f, ci, pl.ds(LANE_DT + g * SSM_R + r, 1), :] for f in range(4)]
                    for r in range(SSM_R)]
            st.append(dict(rows=rows, xb=x_ref[rows, :], cm=cm, sm=sm, rowf=rowf,
                           scores=_dot_nt(cm, bc), bct=bc.astype(F32).T))
        for s_ in st:
            s_["ac_c"] = [_col(s_["sm"][0], lane_iota, LANE_DT + g * SSM_R + r) for r in range(SSM_R)]
            s_["eac_c"] = [_col(s_["sm"][1], lane_iota, LANE_DT + g * SSM_R + r) for r in range(SSM_R)]
        for s_ in st:
            s_["m"] = [(s_["scores"] * jnp.where(
                causal, jnp.exp(jnp.where(causal, s_["ac_c"][r] - s_["rowf"][r][1], 0.0)), 0.0)
                * s_["rowf"][r][0]).astype(BF16) for r in range(SSM_R)]
            s_["bt"] = [(s_["bct"] * s_["rowf"][r][2]).astype(BF16) for r in range(SSM_R)]
        for s_ in st:
            yd, stt, eac, eal = [], [], [], []
            for pr in range(npair):
                xp = s_["xb"][:, pr * LANES:(pr + 1) * LANES]
                r0_, r1_ = 2 * pr, 2 * pr + 1
                yd.append(pick(_dot(s_["m"][r0_], xp), _dot(s_["m"][r1_], xp)))
                stt.append(pick(_dot(s_["bt"][r0_], xp), _dot(s_["bt"][r1_], xp)))
                eac.append(pick(jnp.broadcast_to(s_["eac_c"][r0_], (l, LANES)),
                                jnp.broadcast_to(s_["eac_c"][r1_], (l, LANES))))
                eal.append(pick(s_["rowf"][r0_][3], s_["rowf"][r1_][3]))
            s_["yd"] = jnp.concatenate(yd, axis=1)
            s_["states"] = jnp.concatenate(stt, axis=1)
            s_["eac"] = jnp.concatenate(eac, axis=1)
            s_["eal"] = jnp.concatenate(eal, axis=1)
        for s_ in st:
            y_off = _dot(s_["cm"], h.astype(BF16))
            h = h * s_["eal"] + s_["states"]
            y = (s_["yd"] + y_off * s_["eac"] + dsk * s_["xb"].astype(F32)) \
                * z_ref[s_["rows"], :].astype(F32)
            o_ref[s_["rows"], :] = (_rms(y) * nw).astype(o_ref.dtype)
        h_scr[g] = h
        return carry

    lax.fori_loop(0, nch // cpi, body, 0)


def _ssd(xbc, z, m_fields, m_rows, dskip, norm_w, batch, seq, z_col0):
    bt = xbc.shape[0]
    tbs = MIX_TB
    ntb = seq // tbs
    ng = SSM_GROUPS
    gw = SSM_R * SSM_P
    inner = SSM_HEADS * SSM_P
    bb0 = inner // SSM_N
    cb0 = (inner + ng * SSM_N) // SSM_N
    zb0 = z_col0 // gw
    row = lambda b, t, g: b * ntb + t
    return pl.pallas_call(
        _ssd_kernel,
        grid=(batch, ntb, ng),
        in_specs=[
            pl.BlockSpec((tbs, gw), lambda b, t, g: (row(b, t, g), g)),
            pl.BlockSpec((tbs, SSM_N), lambda b, t, g: (row(b, t, g), bb0 + g)),
            pl.BlockSpec((tbs, SSM_N), lambda b, t, g: (row(b, t, g), cb0 + g)),
            pl.BlockSpec((tbs, gw), lambda b, t, g: (row(b, t, g), zb0 + g)),
            pl.BlockSpec((2, tbs, LANES), lambda b, t, g: (0, row(b, t, g), 0)),
            pl.BlockSpec((4, tbs // SSM_CHUNK, LANES, SSM_CHUNK),
                         lambda b, t, g: (0, row(b, t, g), 0, 0)),
            pl.BlockSpec((1, gw), lambda b, t, g: (0, g)),
            pl.BlockSpec((1, gw), lambda b, t, g: (0, g)),
        ],
        out_specs=pl.BlockSpec((tbs, gw), lambda b, t, g: (row(b, t, g), g)),
        out_shape=jax.ShapeDtypeStruct((bt, inner), BF16),
        scratch_shapes=[pltpu.VMEM((ng, SSM_N, gw), F32)],
        compiler_params=pltpu.CompilerParams(
            dimension_semantics=("arbitrary", "arbitrary", "arbitrary"),
            vmem_limit_bytes=VMEM_LIMIT),
        name="ssd",
    )(xbc, xbc, xbc, z, m_fields, m_rows, dskip, norm_w)


def _merge_kernel(o_ref, y_ref, wg_ref, ws_ref, ga_ref, gb_ref, out_ref):
    a = _dot(o_ref[...], wg_ref[...])
    b = _dot(y_ref[...], ws_ref[...])
    out_ref[...] = (ga_ref[...].astype(F32) * a + gb_ref[...].astype(F32) * b).astype(out_ref.dtype)


def _merge(o, y, wg, ws, ga, gb, tm=512, tn=1024):
    bt, d_in = o.shape
    d = wg.shape[1]
    return pl.pallas_call(
        _merge_kernel,
        grid=(d // tn, bt // tm),
        in_specs=[
            pl.BlockSpec((tm, d_in), lambda j, i: (i, 0)),
            pl.BlockSpec((tm, d_in), lambda j, i: (i, 0)),
            pl.BlockSpec((d_in, tn), lambda j, i: (0, j)),
            pl.BlockSpec((d_in, tn), lambda j, i: (0, j)),
            pl.BlockSpec((tm, tn), lambda j, i: (i, j)),
            pl.BlockSpec((tm, tn), lambda j, i: (i, j)),
        ],
        out_specs=pl.BlockSpec((tm, tn), lambda j, i: (i, j)),
        out_shape=jax.ShapeDtypeStruct((bt, d), BF16),
        compiler_params=pltpu.CompilerParams(
            dimension_semantics=("parallel", "parallel"),
            vmem_limit_bytes=VMEM_LIMIT),
        name="merge",
    )(o, y, wg, ws, ga, gb)


def _oproj_kernel(m_ref, w_ref, h_ref, out_ref):
    out_ref[...] = h_ref[...] + _dot(m_ref[...], w_ref[...])


def _oproj(merged, wo, h, tm=512, tn=1024):
    bt, d_in = merged.shape
    d = wo.shape[1]
    return pl.pallas_call(
        _oproj_kernel,
        grid=(d // tn, bt // tm),
        in_specs=[
            pl.BlockSpec((tm, d_in), lambda j, i: (i, 0)),
            pl.BlockSpec((d_in, tn), lambda j, i: (0, j)),
            pl.BlockSpec((tm, tn), lambda j, i: (i, j)),
        ],
        out_specs=pl.BlockSpec((tm, tn), lambda j, i: (i, j)),
        out_shape=jax.ShapeDtypeStruct((bt, d), F32),
        compiler_params=pltpu.CompilerParams(
            dimension_semantics=("parallel", "parallel"),
            vmem_limit_bytes=VMEM_LIMIT),
        name="oproj",
    )(merged, wo, h)


def _mlp_kernel(h_ref, nw_ref, wu_ref, wd_ref, fw_ref, out_ref, xn_scr, acc_scr, *, final):
    k = pl.program_id(1)

    @pl.when(k == 0)
    def _():
        xn_scr[...] = (_rms(h_ref[...]) * nw_ref[...]).astype(BF16)
        acc_scr[...] = jnp.zeros_like(acc_scr)

    up = _dot(xn_scr[...], wu_ref[...])
    act = jnp.square(jnp.maximum(up, 0.0)).astype(BF16)
    acc_scr[...] += _dot(act, wd_ref[...])

    @pl.when(k == pl.num_programs(1) - 1)
    def _():
        y = h_ref[...] + acc_scr[...]
        if final:
            y = _rms(y) * fw_ref[...]
        out_ref[...] = y


def _mlp(h, norm_w, w_up, w_down, final_w, final, tm=512, th=1024):
    bt, d = h.shape
    hid = w_up.shape[1]
    return pl.pallas_call(
        functools.partial(_mlp_kernel, final=final),
        grid=(bt // tm, hid // th),
        in_specs=[
            pl.BlockSpec((tm, d), lambda i, k: (i, 0)),
            pl.BlockSpec((1, d), lambda i, k: (0, 0)),
            pl.BlockSpec((d, th), lambda i, k: (0, k)),
            pl.BlockSpec((th, d), lambda i, k: (k, 0)),
            pl.BlockSpec((1, d), lambda i, k: (0, 0)),
        ],
        out_specs=pl.BlockSpec((tm, d), lambda i, k: (i, 0)),
        out_shape=jax.ShapeDtypeStruct((bt, d), F32),
        scratch_shapes=[pltpu.VMEM((tm, d), BF16), pltpu.VMEM((tm, d), F32)],
        compiler_params=pltpu.CompilerParams(
            dimension_semantics=("parallel", "arbitrary"),
            vmem_limit_bytes=VMEM_LIMIT),
        name="mlp",
    )(h, norm_w, w_up, w_down, final_w)


def kernel(x, norm1_w, w_in, gdn_conv_w, gdn_a_log, gdn_dt_bias, gdn_norm_w, ssm_conv_w, ssm_conv_b, ssm_a_log, ssm_dt_bias, ssm_d, ssm_norm_w, w_gdn_out, w_ssm_out, w_o, norm2_w, w_up, w_down, final_norm_w):
    batch, seq, d = x.shape
    depth = w_in.shape[0]
    assert depth >= 1
    bt = batch * seq
    gw = GDN_HEADS * GDN_DK
    inner = SSM_HEADS * SSM_P
    conv_ch = inner + 2 * SSM_GROUPS * SSM_N
    sizes = (3 * gw, gw, GDN_HEADS, GDN_HEADS, inner, conv_ch, SSM_HEADS, d, d)
    offs = [0]
    for s in sizes:
        offs.append(offs[-1] + s)
    o_qkv, o_gz, o_gb, o_ga, o_sz, o_xbc, o_dt, o_gate_a, o_gate_b, _ = offs
    no_w = jnp.zeros((CONV_K, max(gw, inner, d)), F32)
    no_b = jnp.zeros((1, max(2 * gw, conv_ch, d)), F32)
    zpad = jnp.zeros((LANES - 2 * GDN_HEADS - SSM_HEADS,), F32)

    h = x.reshape(bt, d)
    for l in range(depth):
        assert o_ga == o_gb + GDN_HEADS
        w_qk, w_v, w_xbc, w_gz, w_sz, w_ga, w_gb, w_small = _wprep(
            w_in[l],
            segs=((o_qkv, 2 * gw), (o_qkv + 2 * gw, gw), (o_xbc, conv_ch), (o_gz, gw), (o_sz, inner),
                  (o_gate_a, d), (o_gate_b, d)),
            smalls=((o_gb, 2 * GDN_HEADS), (o_dt, SSM_HEADS)))

        xn = _norm(h, norm1_w[l][None, :])
        proj = functools.partial(_proj, xn, seq=seq)
        qk = proj(w_qk, gdn_conv_w[l][:, :2 * gw], no_b, conv=True, act="silu",
                  l2norm=True, q_tiles=gw // 1024, name="proj_qk")
        v = proj(w_v, gdn_conv_w[l][:, 2 * gw:], no_b, conv=True, act="silu",
                 l2norm=False, q_tiles=0, name="proj_v")
        xbc = proj(w_xbc, ssm_conv_w[l], ssm_conv_b[l][None, :], conv=True, act="silu",
                   l2norm=False, q_tiles=0, name="proj_xbc")
        gz = proj(w_gz, no_w, no_b, conv=False, act="silu", l2norm=False, q_tiles=0, name="proj_gz")
        sz = proj(w_sz, no_w, no_b, conv=False, act="silu", l2norm=False, q_tiles=0, name="proj_sz")
        ga = proj(w_ga, no_w, no_b, conv=False, act="sigmoid", l2norm=False, q_tiles=0,
                  name="proj_ga")
        gb = proj(w_gb, no_w, no_b, conv=False, act="sigmoid", l2norm=False, q_tiles=0,
                  name="proj_gb")

        params = jnp.zeros((8, LANES), F32)
        params = params.at[0].set(jnp.concatenate(
            [jnp.zeros((GDN_HEADS,), F32), gdn_dt_bias[l], ssm_dt_bias[l], zpad]))
        params = params.at[1].set(jnp.concatenate(
            [jnp.zeros((GDN_HEADS,), F32), gdn_a_log[l], ssm_a_log[l], zpad]))
        g_cols, g_rows, m_cols, m_rows = _smallprep(xn, w_small, params)

        o = _gdn(qk, v, gz, g_cols, g_rows, gdn_norm_w[l][None, :], batch, seq)
        dskip = jnp.repeat(ssm_d[l], SSM_P)[None, :]
        y = _ssd(xbc, sz, m_cols, m_rows, dskip, ssm_norm_w[l][None, :], batch, seq, 0)
        merged = _merge(o, y, w_gdn_out[l].astype(BF16), w_ssm_out[l].astype(BF16), ga, gb)
        h = _oproj(merged, w_o[l].astype(BF16), h)
        h = _mlp(h, norm2_w[l][None, :], w_up[l].astype(BF16), w_down[l].astype(BF16),
                 final_norm_w[None, :], final=(l == depth - 1))
    return h.reshape(batch, seq, d)
```

```python
import functools

import jax
import jax.numpy as jnp
from jax import lax
from jax.experimental import pallas as pl
from jax.experimental.pallas import tpu as pltpu

F32 = jnp.float32
BF16 = jnp.bfloat16
HIGHEST = lax.Precision.HIGHEST
EPS = 1e-6

LANES = 128
CONV_K = 4
GDN_HEADS = 16
GDN_DK = 128
GDN_CHUNK = 128
GDN_HB = 4
GDN_CPI = 4
SSM_HEADS = 32
SSM_P = 64
SSM_GROUPS = 8
SSM_N = 128
SSM_CHUNK = 128
SSM_R = SSM_HEADS // SSM_GROUPS
SSD_CPI = 2
GDN_TB = 512
SSD_TB = 1024
PROJ_SUB = 256
PROJ_ROWS = 128
VMEM_LIMIT = 48 * 1024 * 1024

LANE_BETA = 0
LANE_GDEC = 16
LANE_DT = 32


def _sigmoid(x):
    return 1.0 / (1.0 + jnp.exp(-x))


def _silu(x):
    return x * _sigmoid(x)


def _dot(a, b, precision=None):
    return jnp.dot(a, b, preferred_element_type=F32, precision=precision)


def _dot3(a, b):
    ah = a.astype(BF16)
    al = (a - ah.astype(F32)).astype(BF16)
    bh = b.astype(BF16)
    bl = (b - bh.astype(F32)).astype(BF16)
    return _dot(ah, bh) + (_dot(ah, bl) + _dot(al, bh))


def _dot_nt(a, b):
    return lax.dot_general(a, b, (((1,), (1,)), ((), ())), preferred_element_type=F32)


def _dot_tn(a, b):
    return lax.dot_general(a, b, (((0,), (0,)), ((), ())), preferred_element_type=F32)


def _rms(x):
    return x * lax.rsqrt(jnp.mean(x * x, axis=-1, keepdims=True) + EPS)


def _norm_kernel(x_ref, w_ref, o_ref):
    o_ref[...] = (_rms(x_ref[...]) * w_ref[...]).astype(o_ref.dtype)


def _norm(x2, w, tm=512):
    bt, d = x2.shape
    return pl.pallas_call(
        _norm_kernel,
        grid=(bt // tm,),
        in_specs=[pl.BlockSpec((tm, d), lambda i: (i, 0)), pl.BlockSpec((1, d), lambda i: (0, 0))],
        out_specs=pl.BlockSpec((tm, d), lambda i: (i, 0)),
        out_shape=jax.ShapeDtypeStruct((bt, d), BF16),
        compiler_params=pltpu.CompilerParams(dimension_semantics=("parallel",)),
        name="norm",
    )(x2, w)


def _proj_kernel(xn_ref, w_ref, cw_ref, cb_ref, o_ref, acc_scr, *, conv, act, l2norm, q_tiles, seq):
    j = pl.program_id(0)
    i = pl.program_id(1)
    tm, tn = o_ref.shape
    nsub = tn // PROJ_SUB
    if conv:
        @pl.when((i * tm) % seq == 0)
        def _():
            for s in range(nsub):
                acc_scr[s, 0:8, :] = jnp.zeros((8, PROJ_SUB), F32)
    if l2norm:
        scale = jnp.where(j < q_tiles, GDN_DK ** -0.5, 1.0)

    def matmul(s):
        acc_scr[s, 8:, :] = _dot_nt(xn_ref[...], w_ref[s * PROJ_SUB:(s + 1) * PROJ_SUB, :])

    def epilogue(s):
        cols = slice(s * PROJ_SUB, (s + 1) * PROJ_SUB)
        for rc in range(tm // PROJ_ROWS):
            r0 = rc * PROJ_ROWS
            if conv:
                win = acc_scr[s, r0:r0 + PROJ_ROWS + 8, :]
                prev = pltpu.roll(win, 1, axis=0)
                lo = win * cw_ref[1:2, cols] + prev * cw_ref[0:1, cols]
                y = (win * cw_ref[3:4, cols] + prev * cw_ref[2:3, cols]
                     + pltpu.roll(lo, 2, axis=0))[8:] + cb_ref[:, cols]
            else:
                y = acc_scr[s, 8 + r0:8 + r0 + PROJ_ROWS, :]
            y = _silu(y) if act == "silu" else _sigmoid(y)
            if l2norm:
                parts = []
                for hh in range(PROJ_SUB // GDN_DK):
                    yh = y[:, hh * GDN_DK:(hh + 1) * GDN_DK]
                    parts.append(
                        yh * (lax.rsqrt(jnp.sum(yh * yh, axis=-1, keepdims=True) + EPS) * scale))
                y = jnp.concatenate(parts, axis=1)
            o_ref[r0:r0 + PROJ_ROWS, cols] = y.astype(o_ref.dtype)
        if conv:
            acc_scr[s, 0:8, :] = acc_scr[s, tm:tm + 8, :]

    matmul(0)
    for s in range(nsub):
        if s + 1 < nsub:
            matmul(s + 1)
        epilogue(s)


def _proj(xn, wt, cw, cb, *, conv, act, l2norm, q_tiles, seq, name, tm=1024, tn=1024):
    bt, d = xn.shape
    n = wt.shape[0]
    assert seq % tm == 0 and n % tn == 0
    body = functools.partial(_proj_kernel, conv=conv, act=act, l2norm=l2norm, q_tiles=q_tiles, seq=seq)
    return pl.pallas_call(
        body,
        grid=(n // tn, bt // tm),
        in_specs=[
            pl.BlockSpec((tm, d), lambda j, i: (i, 0)),
            pl.BlockSpec((tn, d), lambda j, i: (j, 0)),
            pl.BlockSpec((CONV_K, tn), lambda j, i: (0, j)),
            pl.BlockSpec((1, tn), lambda j, i: (0, j)),
        ],
        out_specs=pl.BlockSpec((tm, tn), lambda j, i: (i, j)),
        out_shape=jax.ShapeDtypeStruct((bt, n), BF16),
        scratch_shapes=[pltpu.VMEM((tn // PROJ_SUB, tm + 8, PROJ_SUB), F32)],
        compiler_params=pltpu.CompilerParams(
            dimension_semantics=("arbitrary", "arbitrary"),
            vmem_limit_bytes=VMEM_LIMIT),
        name=name,
    )(xn, wt, cw, cb)


def _smallprep_kernel(xn_ref, w_ref, p_ref, gc_ref, gr_ref, mc_ref, mr_ref):
    tm = xn_ref.shape[0]
    bias = p_ref[0:1, :]
    neg_a = -jnp.exp(p_ref[1:2, :])
    ii = lax.broadcasted_iota(jnp.int32, (LANES, LANES), 0)
    jj = lax.broadcasted_iota(jnp.int32, (LANES, LANES), 1)
    tril = jnp.where(ii >= jj, 1.0, 0.0).astype(F32)
    ones = jnp.ones((LANES, LANES), F32)
    logits = _dot_nt(xn_ref[...], w_ref[...])
    for s in range(tm // LANES):
        rows = slice(s * LANES, (s + 1) * LANES)
        x = logits[rows, :]
        xb = x + bias
        sp = jnp.maximum(xb, 0.0) + jnp.log1p(jnp.exp(-jnp.abs(xb)))
        dec = neg_a * sp
        cum = _dot(tril, dec, HIGHEST)
        last = _dot(ones, dec, HIGHEST)
        beta = _sigmoid(x)
        ecum = jnp.exp(cum)
        etail = jnp.exp(last - cum)
        elast = jnp.exp(last)
        for f, val in enumerate((beta, cum, ecum, etail)):
            gc_ref[f, rows, :] = val
        for f, val in enumerate((beta, cum, ecum, elast)):
            gr_ref[f, s] = val.T
        for f, val in enumerate((cum, ecum)):
            mc_ref[f, rows, :] = val
        for f, val in enumerate((sp, cum, sp * etail, elast)):
            mr_ref[f, s] = val.T


def _smallprep(xn, w_small, params, tm=512):
    bt, d = xn.shape
    nb = tm // LANES

    def cspec(nf):
        return pl.BlockSpec((nf, tm, LANES), lambda i: (0, i, 0))

    def rspec(nf):
        return pl.BlockSpec((nf, nb, LANES, LANES), lambda i: (0, i, 0, 0))

    def cshape(nf):
        return jax.ShapeDtypeStruct((nf, bt, LANES), F32)

    def rshape(nf):
        return jax.ShapeDtypeStruct((nf, bt // LANES, LANES, LANES), F32)

    return pl.pallas_call(
        _smallprep_kernel,
        grid=(bt // tm,),
        in_specs=[pl.BlockSpec((tm, d), lambda i: (i, 0)),
                  pl.BlockSpec((LANES, d), lambda i: (0, 0)),
                  pl.BlockSpec((8, LANES), lambda i: (0, 0))],
        out_specs=[cspec(4), rspec(4), cspec(2), rspec(4)],
        out_shape=[cshape(4), rshape(4), cshape(2), rshape(4)],
        compiler_params=pltpu.CompilerParams(dimension_semantics=("parallel",)),
        name="smallprep",
    )(xn, w_small, params)


def _col(field, lane_iota, lane):
    return jnp.sum(jnp.where(lane_iota == lane, field, 0.0), axis=-1, keepdims=True)


def _gdn_kernel(q_ref, k_ref, v_ref, z_ref, sm_ref, smt_ref, nw_ref,
                o_ref,
                s_scr, qp_scr, op_scr, phi_scr, psi_scr, egl_scr):
    tb = pl.program_id(1)
    hg = pl.program_id(2)
    tbs = q_ref.shape[0]
    nch = tbs // GDN_CHUNK
    c = GDN_CHUNK
    dk = GDN_DK

    @pl.when(tb == 0)
    def _():
        s_scr[pl.ds(hg * GDN_HB, GDN_HB)] = jnp.zeros((GDN_HB, dk, dk), F32)

    ii = lax.broadcasted_iota(jnp.int32, (c, c), 0)
    jj = lax.broadcasted_iota(jnp.int32, (c, c), 1)
    causal = ii >= jj
    strict = ii > jj
    nlev = 4
    blk = {w: (ii // w) == (jj // w) for w in (16, 32, 64, 128) if w < c}
    ring = {w: jnp.logical_and((ii // (2 * w)) == (jj // (2 * w)), (ii // w) != (jj // w))
            for w in blk}
    lane_iota = lax.broadcasted_iota(jnp.int32, (c, LANES), 1)
    q_lane = lax.broadcasted_iota(jnp.int32, (16, c), 1)
    q_blk = q_lane // 16
    q_eye = jnp.where(q_lane % 16 == lax.broadcasted_iota(jnp.int32, (16, c), 0), 1.0, 0.0).astype(F32)
    cpi = GDN_CPI

    def phase_a(it, carry):
        st = []
        for cc in range(cpi):
            ci = it * cpi + cc
            rows = pl.ds(pl.multiple_of(ci * c, c), c)
            sm = sm_ref[:, rows, :]
            for r in range(GDN_HB):
                h = hg * GDN_HB + r
                cols = slice(r * dk, (r + 1) * dk)
                gc_r, egl_r = (smt_ref[f, ci, pl.ds(LANE_GDEC + h, 1), :] for f in (1, 3))
                st.append(dict(
                    rows=rows, ci=ci, r=r, h=h, sm=sm, qb=q_ref[rows, cols], kb=k_ref[rows, cols],
                    vb=v_ref[rows, cols], gc_r=gc_r, egl=jnp.broadcast_to(egl_r, (8, dk))))
        grams = [_dot_nt(jnp.concatenate([s_["kb"], s_["qb"]], axis=0), s_["kb"]) for s_ in st]
        for f, name, lane0 in ((0, "beta_c", LANE_BETA), (1, "gc_c", LANE_GDEC),
                               (2, "egc_c", LANE_GDEC), (3, "ekl_c", LANE_GDEC)):
            for s_ in st:
                s_[name] = _col(s_["sm"][f], lane_iota, lane0 + s_["h"])
        decs = [jnp.where(causal, jnp.exp(jnp.where(causal, s_["gc_c"] - s_["gc_r"], 0.0)), 0.0)
                for s_ in st]
        for s_, gram, dec in zip(st, grams, decs):
            s_["a"] = jnp.where(strict, gram[:c] * s_["beta_c"] * dec, 0.0)
            s_["attn"] = (gram[c:] * dec).astype(BF16)
        for s_ in st:
            s_["qg"] = s_["qb"].astype(F32) * s_["egc_c"]
            s_["kd"] = (s_["kb"].astype(F32) * s_["ekl_c"]).astype(BF16)

        widths = sorted(blk)

        def expand(q):
            return jnp.where(blk[16], jnp.concatenate([q] * (c // 16), axis=0), 0.0)

        qm, qp = [], []
        for s_ in st:
            acc = None
            for b in range(c // 16):
                term = jnp.where(q_blk == b, s_["a"][16 * b:16 * (b + 1), :], 0.0)
                acc = term if acc is None else acc + term
            qm.append(-acc)
            qp.append(q_eye)
        for lev in range(nlev):
            wm = [expand(m_) for m_ in qm]
            if lev + 1 < nlev:
                outs = [_dot3(jnp.concatenate([m_, p_], axis=0), w_) for m_, p_, w_ in zip(qm, qp, wm)]
                qm = [o_[:16] for o_ in outs]
                qp = [p_ + o_[16:] for p_, o_ in zip(qp, outs)]
            else:
                qp = [p_ + _dot3(p_, w_) for p_, w_ in zip(qp, wm)]
        tinv = [expand(p_) for p_ in qp]
        for w_blk in widths:
            nb = c // (2 * w_blk)
            eb = [jnp.where(ring[w_blk], s_["a"], 0.0).astype(BF16) for s_ in st]
            low = [slice((2 * b + 1) * w_blk, (2 * b + 2) * w_blk) for b in range(nb)]
            upp = [slice(2 * b * w_blk, (2 * b + 1) * w_blk) for b in range(nb)]
            tl = [jnp.concatenate([t_[sl] for sl in low], axis=0) for t_ in tinv]
            tb_ = [t_.astype(BF16) for t_ in tinv]
            te = [_dot(l_.astype(BF16), e_).astype(BF16) for l_, e_ in zip(tl, eb)]
            tl = [l_ - _dot(e_, b_) for l_, e_, b_ in zip(tl, te, tb_)]
            tinv = [jnp.concatenate(
                [piece for b in range(nb) for piece in (t_[upp[b]], l_[b * w_blk:(b + 1) * w_blk])],
                axis=0) for t_, l_ in zip(tinv, tl)]

        rhs = [jnp.concatenate(
            [s_["kb"].astype(F32) * (s_["beta_c"] * s_["egc_c"]), s_["vb"].astype(F32) * s_["beta_c"]],
            axis=1).astype(BF16) for s_ in st]
        wus = [_dot(t_.astype(BF16), r_).astype(BF16) for t_, r_ in zip(tinv, rhs)]
        aws = [_dot(s_["attn"], wu_) for s_, wu_ in zip(st, wus)]
        kws = [_dot_tn(s_["kd"], wu_) for s_, wu_ in zip(st, wus)]
        for s_, aw, kw in zip(st, aws, kws):
            r, rows, ci = s_["r"], s_["rows"], s_["ci"]
            qp_scr[r, rows, :] = (s_["qg"] - aw[:, :dk]).astype(BF16)
            op_scr[r, rows, :] = aw[:, dk:]
            phi_scr[r, ci] = (-kw[:, :dk]).astype(BF16)
            psi_scr[r, ci] = kw[:, dk:]
            egl_scr[r, ci] = s_["egl"]
        return carry

    lax.fori_loop(0, nch // cpi, phase_a, 0)

    nw = nw_ref[...]

    heads = range(GDN_HB)
    ss = [s_scr[hg * GDN_HB + r] for r in heads]
    for ci in range(nch):
        rows = slice(ci * c, (ci + 1) * c)
        sb = [s_.astype(BF16) for s_ in ss]
        ds = [_dot(phi_scr[r, ci], sb[r]) for r in heads]
        ss = [ss[r] * egl_scr[r, ci, 0:1, :] + ds[r] + psi_scr[r, ci] for r in heads]
        os_ = [_dot(qp_scr[r, rows, :], sb[r]) + op_scr[r, rows, :] for r in heads]
        ons = [_rms(o) * nw for o in os_]
        for r in heads:
            cols = slice(r * dk, (r + 1) * dk)
            o_ref[rows, cols] = (ons[r] * z_ref[rows, cols].astype(F32)).astype(o_ref.dtype)
    for r in heads:
        s_scr[hg * GDN_HB + r] = ss[r]


def _gdn(qk, v, z, g_fields, g_rows, norm_w, batch, seq):
    bt = qk.shape[0]
    tbs = GDN_TB
    ntb = seq // tbs
    nhg = GDN_HEADS // GDN_HB
    cw = GDN_HB * GDN_DK
    width = GDN_HEADS * GDN_DK
    row = lambda b, t, h: b * ntb + t
    return pl.pallas_call(
        _gdn_kernel,
        grid=(batch, ntb, nhg),
        in_specs=[
            pl.BlockSpec((tbs, cw), lambda b, t, h: (row(b, t, h), h)),
            pl.BlockSpec((tbs, cw), lambda b, t, h: (row(b, t, h), nhg + h)),
            pl.BlockSpec((tbs, cw), lambda b, t, h: (row(b, t, h), h)),
            pl.BlockSpec((tbs, cw), lambda b, t, h: (row(b, t, h), h)),
            pl.BlockSpec((4, tbs, LANES), lambda b, t, h: (0, row(b, t, h), 0)),
            pl.BlockSpec((4, tbs // LANES, LANES, LANES), lambda b, t, h: (0, row(b, t, h), 0, 0)),
            pl.BlockSpec((1, GDN_DK), lambda b, t, h: (0, 0)),
        ],
        out_specs=pl.BlockSpec((tbs, cw), lambda b, t, h: (row(b, t, h), h)),
        out_shape=jax.ShapeDtypeStruct((bt, width), BF16),
        scratch_shapes=[
            pltpu.VMEM((GDN_HEADS, GDN_DK, GDN_DK), F32),
            pltpu.VMEM((GDN_HB, tbs, GDN_DK), BF16),
            pltpu.VMEM((GDN_HB, tbs, GDN_DK), F32),
            pltpu.VMEM((GDN_HB, tbs // GDN_CHUNK, GDN_DK, GDN_DK), BF16),
            pltpu.VMEM((GDN_HB, tbs // GDN_CHUNK, GDN_DK, GDN_DK), F32),
            pltpu.VMEM((GDN_HB, tbs // GDN_CHUNK, 8, GDN_DK), F32),
        ],
        compiler_params=pltpu.CompilerParams(
            dimension_semantics=("arbitrary", "arbitrary", "arbitrary"),
            vmem_limit_bytes=VMEM_LIMIT),
        name="gdn",
    )(qk, qk, v, z, g_fields, g_rows, norm_w)


def _ssd_kernel(x_ref, b_ref, c_ref, z_ref, sm_ref, smt_ref, dsk_ref, nw_ref,
                o_ref,
                h_scr):
    tb = pl.program_id(1)
    g = pl.program_id(2)
    tbs = x_ref.shape[0]
    l = SSM_CHUNK
    p = SSM_P
    nch = tbs // l

    @pl.when(tb == 0)
    def _():
        h_scr[g] = jnp.zeros(h_scr.shape[1:], F32)

    ii = lax.broadcasted_iota(jnp.int32, (l, l), 0)
    jj = lax.broadcasted_iota(jnp.int32, (l, l), 1)
    causal = ii >= jj
    lane_iota = lax.broadcasted_iota(jnp.int32, (l, LANES), 1)
    dsk = dsk_ref[...]
    nw = nw_ref[...]

    first_of_pair = lane_iota < p
    npair = SSM_R // 2
    cpi = SSD_CPI

    def pick(a0, a1):
        return jnp.where(first_of_pair[:a0.shape[0]], a0, a1)

    def body(it, carry):
        h = h_scr[g]
        st = []
        for cc_ in range(cpi):
            ci = it * cpi + cc_
            rows = pl.ds(pl.multiple_of(ci * l, l), l)
            bc = b_ref[rows, :]
            cm = c_ref[rows, :]
            sm = sm_ref[:, rows, :]
            rowf = [[smt_ref[f, ci, pl.ds(LANE_DT + g * SSM_R + r, 1), :] for f in range(4)]
                    for r in range(SSM_R)]
            st.append(dict(rows=rows, xb=x_ref[rows, :], cm=cm, sm=sm, rowf=rowf,
                           scores=_dot_nt(cm, bc), bct=bc.astype(F32).T))
        for s_ in st:
            s_["ac_c"] = [_col(s_["sm"][0], lane_iota, LANE_DT + g * SSM_R + r) for r in range(SSM_R)]
            s_["eac_c"] = [_col(s_["sm"][1], lane_iota, LANE_DT + g * SSM_R + r) for r in range(SSM_R)]
        for s_ in st:
            s_["m"] = [(s_["scores"] * jnp.where(
                causal, jnp.exp(jnp.where(causal, s_["ac_c"][r] - s_["rowf"][r][1], 0.0)), 0.0)
                * s_["rowf"][r][0]).astype(BF16) for r in range(SSM_R)]
            s_["bt"] = [(s_["bct"] * s_["rowf"][r][2]).astype(BF16) for r in range(SSM_R)]
        for s_ in st:
            yd, stt, eac, eal = [], [], [], []
            for pr in range(npair):
                xp = s_["xb"][:, pr * LANES:(pr + 1) * LANES]
                r0_, r1_ = 2 * pr, 2 * pr + 1
                yd.append(pick(_dot(s_["m"][r0_], xp), _dot(s_["m"][r1_], xp)))
                stt.append(pick(_dot(s_["bt"][r0_], xp), _dot(s_["bt"][r1_], xp)))
                eac.append(pick(jnp.broadcast_to(s_["eac_c"][r0_], (l, LANES)),
                                jnp.broadcast_to(s_["eac_c"][r1_], (l, LANES))))
                eal.append(pick(s_["rowf"][r0_][3], s_["rowf"][r1_][3]))
            s_["yd"] = jnp.concatenate(yd, axis=1)
            s_["states"] = jnp.concatenate(stt, axis=1)
            s_["eac"] = jnp.concatenate(eac, axis=1)
            s_["eal"] = jnp.concatenate(eal, axis=1)
        for s_ in st:
            y_off = _dot(s_["cm"], h.astype(BF16))
            h = h * s_["eal"] + s_["states"]
            y = (s_["yd"] + y_off * s_["eac"] + dsk * s_["xb"].astype(F32)) \
                * z_ref[s_["rows"], :].astype(F32)
            o_ref[s_["rows"], :] = (_rms(y) * nw).astype(o_ref.dtype)
        h_scr[g] = h
        return carry

    lax.fori_loop(0, nch // cpi, body, 0)


def _ssd(xbc, z, m_fields, m_rows, dskip, norm_w, batch, seq, z_col0):
    bt = xbc.shape[0]
    tbs = SSD_TB
    ntb = seq // tbs
    ng = SSM_GROUPS
    gw = SSM_R * SSM_P
    inner = SSM_HEADS * SSM_P
    bb0 = inner // SSM_N
    cb0 = (inner + ng * SSM_N) // SSM_N
    zb0 = z_col0 // gw
    row = lambda b, t, g: b * ntb + t
    return pl.pallas_call(
        _ssd_kernel,
        grid=(batch, ntb, ng),
        in_specs=[
            pl.BlockSpec((tbs, gw), lambda b, t, g: (row(b, t, g), g)),
            pl.BlockSpec((tbs, SSM_N), lambda b, t, g: (row(b, t, g), bb0 + g)),
            pl.BlockSpec((tbs, SSM_N), lambda b, t, g: (row(b, t, g), cb0 + g)),
            pl.BlockSpec((tbs, gw), lambda b, t, g: (row(b, t, g), zb0 + g)),
            pl.BlockSpec((2, tbs, LANES), lambda b, t, g: (0, row(b, t, g), 0)),
            pl.BlockSpec((4, tbs // SSM_CHUNK, LANES, SSM_CHUNK),
                         lambda b, t, g: (0, row(b, t, g), 0, 0)),
            pl.BlockSpec((1, gw), lambda b, t, g: (0, g)),
            pl.BlockSpec((1, gw), lambda b, t, g: (0, g)),
        ],
        out_specs=pl.BlockSpec((tbs, gw), lambda b, t, g: (row(b, t, g), g)),
        out_shape=jax.ShapeDtypeStruct((bt, inner), BF16),
        scratch_shapes=[pltpu.VMEM((ng, SSM_N, gw), F32)],
        compiler_params=pltpu.CompilerParams(
            dimension_semantics=("arbitrary", "arbitrary", "arbitrary"),
            vmem_limit_bytes=VMEM_LIMIT),
        name="ssd",
    )(xbc, xbc, xbc, z, m_fields, m_rows, dskip, norm_w)


def _merge_kernel(o_ref, y_ref, wg_ref, ws_ref, ga_ref, gb_ref, out_ref):
    a = _dot(o_ref[...], wg_ref[...])
    b = _dot(y_ref[...], ws_ref[...])
    out_ref[...] = (ga_ref[...].astype(F32) * a + gb_ref[...].astype(F32) * b).astype(out_ref.dtype)


def _merge(o, y, wg, ws, ga, gb, tm=512, tn=1024):
    bt, d_in = o.shape
    d = wg.shape[1]
    return pl.pallas_call(
        _merge_kernel,
        grid=(d // tn, bt // tm),
        in_specs=[
            pl.BlockSpec((tm, d_in), lambda j, i: (i, 0)),
            pl.BlockSpec((tm, d_in), lambda j, i: (i, 0)),
            pl.BlockSpec((d_in, tn), lambda j, i: (0, j)),
            pl.BlockSpec((d_in, tn), lambda j, i: (0, j)),
            pl.BlockSpec((tm, tn), lambda j, i: (i, j)),
            pl.BlockSpec((tm, tn), lambda j, i: (i, j)),
        ],
        out_specs=pl.BlockSpec((tm, tn), lambda j, i: (i, j)),
        out_shape=jax.ShapeDtypeStruct((bt, d), BF16),
        compiler_params=pltpu.CompilerParams(
            dimension_semantics=("parallel", "parallel"),
            vmem_limit_bytes=VMEM_LIMIT),
        name="merge",
    )(o, y, wg, ws, ga, gb)


def _oproj_kernel(m_ref, w_ref, h_ref, out_ref):
    out_ref[...] = h_ref[...] + _dot(m_ref[...], w_ref[...])


def _oproj(merged, wo, h, tm=512, tn=1024):
    bt, d_in = merged.shape
    d = wo.shape[1]
    return pl.pallas_call(
        _oproj_kernel,
        grid=(d // tn, bt // tm),
        in_specs=[
            pl.BlockSpec((tm, d_in), lambda j, i: (i, 0)),
            pl.BlockSpec((d_in, tn), lambda j, i: (0, j)),
            pl.BlockSpec((tm, tn), lambda j, i: (i, j)),
        ],
        out_specs=pl.BlockSpec((tm, tn), lambda j, i: (i, j)),
        out_shape=jax.ShapeDtypeStruct((bt, d), F32),
        compiler_params=pltpu.CompilerParams(
            dimension_semantics=("parallel", "parallel"),
            vmem_limit_bytes=VMEM_LIMIT),
        name="oproj",
    )(merged, wo, h)


def _mlp_kernel(h_ref, nw_ref, wu_ref, wd_ref, fw_ref, out_ref, xn_scr, acc_scr, *, final):
    k = pl.program_id(1)

    @pl.when(k == 0)
    def _():
        xn_scr[...] = (_rms(h_ref[...]) * nw_ref[...]).astype(BF16)
        acc_scr[...] = jnp.zeros_like(acc_scr)

    up = _dot(xn_scr[...], wu_ref[...])
    act = jnp.square(jnp.maximum(up, 0.0)).astype(BF16)
    acc_scr[...] += _dot(act, wd_ref[...])

    @pl.when(k == pl.num_programs(1) - 1)
    def _():
        y = h_ref[...] + acc_scr[...]
        if final:
            y = _rms(y) * fw_ref[...]
        out_ref[...] = y


def _mlp(h, norm_w, w_up, w_down, final_w, final, tm=512, th=1024):
    bt, d = h.shape
    hid = w_up.shape[1]
    return pl.pallas_call(
        functools.partial(_mlp_kernel, final=final),
        grid=(bt // tm, hid // th),
        in_specs=[
            pl.BlockSpec((tm, d), lambda i, k: (i, 0)),
            pl.BlockSpec((1, d), lambda i, k: (0, 0)),
            pl.BlockSpec((d, th), lambda i, k: (0, k)),
            pl.BlockSpec((th, d), lambda i, k: (k, 0)),
            pl.BlockSpec((1, d), lambda i, k: (0, 0)),
        ],
        out_specs=pl.BlockSpec((tm, d), lambda i, k: (i, 0)),
        out_shape=jax.ShapeDtypeStruct((bt, d), F32),
        scratch_shapes=[pltpu.VMEM((tm, d), BF16), pltpu.VMEM((tm, d), F32)],
        compiler_params=pltpu.CompilerParams(
            dimension_semantics=("parallel", "arbitrary"),
            vmem_limit_bytes=VMEM_LIMIT),
        name="mlp",
    )(h, norm_w, w_up, w_down, final_w)


def kernel(x, norm1_w, w_in, gdn_conv_w, gdn_a_log, gdn_dt_bias, gdn_norm_w, ssm_conv_w, ssm_conv_b, ssm_a_log, ssm_dt_bias, ssm_d, ssm_norm_w, w_gdn_out, w_ssm_out, w_o, norm2_w, w_up, w_down, final_norm_w):
    batch, seq, d = x.shape
    depth = w_in.shape[0]
    assert depth >= 1
    bt = batch * seq
    gw = GDN_HEADS * GDN_DK
    inner = SSM_HEADS * SSM_P
    conv_ch = inner + 2 * SSM_GROUPS * SSM_N
    sizes = (3 * gw, gw, GDN_HEADS, GDN_HEADS, inner, conv_ch, SSM_HEADS, d, d)
    offs = [0]
    for s in sizes:
        offs.append(offs[-1] + s)
    o_qkv, o_gz, o_gb, o_ga, o_sz, o_xbc, o_dt, o_gate_a, o_gate_b, _ = offs
    no_w = jnp.zeros((CONV_K, max(gw, inner, d)), F32)
    no_b = jnp.zeros((1, max(2 * gw, conv_ch, d)), F32)
    zpad = jnp.zeros((LANES - 2 * GDN_HEADS - SSM_HEADS,), F32)

    h = x.reshape(bt, d)
    for l in range(depth):
        wt = jnp.swapaxes(w_in[l], 0, 1)

        def seg(a, n):
            return wt[a:a + n].astype(BF16)

        w_qk, w_v, w_xbc = seg(o_qkv, 2 * gw), seg(o_qkv + 2 * gw, gw), seg(o_xbc, conv_ch)
        w_gz, w_sz, w_ga, w_gb = seg(o_gz, gw), seg(o_sz, inner), seg(o_gate_a, d), seg(o_gate_b, d)
        w_small = jnp.concatenate(
            [wt[o_gb:o_gb + GDN_HEADS], wt[o_ga:o_ga + GDN_HEADS], wt[o_dt:o_dt + SSM_HEADS],
             jnp.zeros((LANES - 2 * GDN_HEADS - SSM_HEADS, d), wt.dtype)], axis=0).astype(BF16)

        xn = _norm(h, norm1_w[l][None, :])
        proj = functools.partial(_proj, xn, seq=seq)
        qk = proj(w_qk, gdn_conv_w[l][:, :2 * gw], no_b, conv=True, act="silu",
                  l2norm=True, q_tiles=gw // 1024, name="proj_qk")
        v = proj(w_v, gdn_conv_w[l][:, 2 * gw:], no_b, conv=True, act="silu",
                 l2norm=False, q_tiles=0, name="proj_v")
        xbc = proj(w_xbc, ssm_conv_w[l], ssm_conv_b[l][None, :], conv=True, act="silu",
                   l2norm=False, q_tiles=0, name="proj_xbc")
        gz = proj(w_gz, no_w, no_b, conv=False, act="silu", l2norm=False, q_tiles=0, name="proj_gz")
        sz = proj(w_sz, no_w, no_b, conv=False, act="silu", l2norm=False, q_tiles=0, name="proj_sz")
        ga = proj(w_ga, no_w, no_b, conv=False, act="sigmoid", l2norm=False, q_tiles=0,
                  name="proj_ga")
        gb = proj(w_gb, no_w, no_b, conv=False, act="sigmoid", l2norm=False, q_tiles=0,
                  name="proj_gb")

        params = jnp.zeros((8, LANES), F32)
        params = params.at[0].set(jnp.concatenate(
            [jnp.zeros((GDN_HEADS,), F32), gdn_dt_bias[l], ssm_dt_bias[l], zpad]))
        params = params.at[1].set(jnp.concatenate(
            [jnp.zeros((GDN_HEADS,), F32), gdn_a_log[l], ssm_a_log[l], zpad]))
        g_cols, g_rows, m_cols, m_rows = _smallprep(xn, w_small, params)

        o = _gdn(qk, v, gz, g_cols, g_rows, gdn_norm_w[l][None, :], batch, seq)
        dskip = jnp.repeat(ssm_d[l], SSM_P)[None, :]
        y = _ssd(xbc, sz, m_cols, m_rows, dskip, ssm_norm_w[l][None, :], batch, seq, 0)
        merged = _merge(o, y, w_gdn_out[l].astype(BF16), w_ssm_out[l].astype(BF16), ga, gb)
        h = _oproj(merged, w_o[l].astype(BF16), h)
        h = _mlp(h, norm2_w[l][None, :], w_up[l].astype(BF16), w_down[l].astype(BF16),
                 final_norm_w[None, :], final=(l == depth - 1))
    return h.reshape(batch, seq, d)
```

```python
import functools

import jax
import jax.numpy as jnp
from jax import lax
from jax.experimental import pallas as pl
from jax.experimental.pallas import tpu as pltpu

F32 = jnp.float32
BF16 = jnp.bfloat16
HIGHEST = lax.Precision.HIGHEST
EPS = 1e-6

LANES = 128
CONV_K = 4
GDN_HEADS = 16
GDN_DK = 128
GDN_CHUNK = 128
GDN_HB = 4
GDN_CPI = 4
SSM_HEADS = 32
SSM_P = 64
SSM_GROUPS = 8
SSM_N = 128
SSM_CHUNK = 128
SSM_R = SSM_HEADS // SSM_GROUPS
SSD_CPI = 2
GDN_TB = 512
SSD_TB = 1024
PROJ_TN = 2048
PROJ_SUB = 256
PROJ_ROWS = 64
PROJ_MSPLIT = 2
VMEM_LIMIT = 48 * 1024 * 1024

LANE_BETA = 0
LANE_GDEC = 16
LANE_DT = 32


def _sigmoid(x):
    return 1.0 / (1.0 + jnp.exp(-x))


def _silu(x):
    return x * _sigmoid(x)


def _dot(a, b, precision=None):
    return jnp.dot(a, b, preferred_element_type=F32, precision=precision)


def _dot3(a, b):
    ah = a.astype(BF16)
    al = (a - ah.astype(F32)).astype(BF16)
    bh = b.astype(BF16)
    bl = (b - bh.astype(F32)).astype(BF16)
    return _dot(ah, bh) + (_dot(ah, bl) + _dot(al, bh))


def _dot_nt(a, b):
    return lax.dot_general(a, b, (((1,), (1,)), ((), ())), preferred_element_type=F32)


def _dot_tn(a, b):
    return lax.dot_general(a, b, (((0,), (0,)), ((), ())), preferred_element_type=F32)


def _rms(x):
    return x * lax.rsqrt(jnp.mean(x * x, axis=-1, keepdims=True) + EPS)


def _norm_kernel(x_ref, w_ref, o_ref):
    o_ref[...] = (_rms(x_ref[...]) * w_ref[...]).astype(o_ref.dtype)


def _norm(x2, w, tm=512):
    bt, d = x2.shape
    return pl.pallas_call(
        _norm_kernel,
        grid=(bt // tm,),
        in_specs=[pl.BlockSpec((tm, d), lambda i: (i, 0)), pl.BlockSpec((1, d), lambda i: (0, 0))],
        out_specs=pl.BlockSpec((tm, d), lambda i: (i, 0)),
        out_shape=jax.ShapeDtypeStruct((bt, d), BF16),
        compiler_params=pltpu.CompilerParams(dimension_semantics=("parallel",)),
        name="norm",
    )(x2, w)


def _proj_kernel(xn_ref, w_ref, cw_ref, cb_ref, o_ref, *acc_scrs, conv, act, l2norm, q_tiles, seq):
    j = pl.program_id(0)
    i = pl.program_id(1)
    tm, tn = o_ref.shape
    nsub = tn // PROJ_SUB
    if conv:
        @pl.when((i * tm) % seq == 0)
        def _():
            for s in range(nsub):
                acc_scrs[s][0:8, :] = jnp.zeros((8, PROJ_SUB), F32)
    if l2norm:
        scale = jnp.where(j < q_tiles, GDN_DK ** -0.5, 1.0)
    z0 = pl.multiple_of(jnp.minimum(i, 0), 8)

    mh = tm // PROJ_MSPLIT

    def matmul(s, m):
        acc_scrs[s][8 + m * mh:8 + (m + 1) * mh, :] = _dot_nt(
            xn_ref[m * mh:(m + 1) * mh, :], w_ref[s * PROJ_SUB:(s + 1) * PROJ_SUB, :])

    def epilogue(s, m):
        acc_scr = acc_scrs[s]
        pieces = [(rc * PROJ_ROWS, lc * LANES)
                  for rc in range(m * mh // PROJ_ROWS, (m + 1) * mh // PROJ_ROWS)
                  for lc in range(PROJ_SUB // LANES)]
        for r0, l0 in pieces:
            sub = slice(l0, l0 + LANES)
            cols = slice(s * PROJ_SUB + l0, s * PROJ_SUB + l0 + LANES)
            if conv:
                win = acc_scr[pl.ds(z0 + r0, PROJ_ROWS + 8), sub]
                prev = pltpu.roll(win, 1, axis=0)
                lo = win * cw_ref[1:2, cols] + prev * cw_ref[0:1, cols]
                y = (win * cw_ref[3:4, cols] + prev * cw_ref[2:3, cols]
                     + pltpu.roll(lo, 2, axis=0))[8:] + cb_ref[:, cols]
            else:
                y = acc_scr[pl.ds(z0 + 8 + r0, PROJ_ROWS), sub]
            y = _silu(y) if act == "silu" else _sigmoid(y)
            if l2norm:
                y = y * (lax.rsqrt(jnp.sum(y * y, axis=-1, keepdims=True) + EPS) * scale)
            o_ref[r0:r0 + PROJ_ROWS, cols] = y.astype(o_ref.dtype)
        if conv and m == PROJ_MSPLIT - 1:
            acc_scr[0:8, :] = acc_scr[tm:tm + 8, :]

    units = [(s, m) for s in range(nsub) for m in range(PROJ_MSPLIT)]
    ahead = 2
    for u in units[:ahead]:
        matmul(*u)
    for n_, u in enumerate(units):
        if n_ + ahead < len(units):
            matmul(*units[n_ + ahead])
        epilogue(*u)


def _proj(xn, wt, cw, cb, *, conv, act, l2norm, q_tiles, seq, name, tm=1024, tn=PROJ_TN):
    bt, d = xn.shape
    n = wt.shape[0]
    assert seq % tm == 0 and n % tn == 0
    body = functools.partial(_proj_kernel, conv=conv, act=act, l2norm=l2norm, q_tiles=q_tiles, seq=seq)
    return pl.pallas_call(
        body,
        grid=(n // tn, bt // tm),
        in_specs=[
            pl.BlockSpec((tm, d), lambda j, i: (i, 0)),
            pl.BlockSpec((tn, d), lambda j, i: (j, 0)),
            pl.BlockSpec((CONV_K, tn), lambda j, i: (0, j)),
            pl.BlockSpec((1, tn), lambda j, i: (0, j)),
        ],
        out_specs=pl.BlockSpec((tm, tn), lambda j, i: (i, j)),
        out_shape=jax.ShapeDtypeStruct((bt, n), BF16),
        scratch_shapes=[pltpu.VMEM((tm + 8, PROJ_SUB), F32) for _ in range(tn // PROJ_SUB)],
        compiler_params=pltpu.CompilerParams(
            dimension_semantics=("arbitrary", "arbitrary"),
            vmem_limit_bytes=VMEM_LIMIT),
        name=name,
    )(xn, wt, cw, cb)


def _smallprep_kernel(xn_ref, w_ref, p_ref, gc_ref, gr_ref, mc_ref, mr_ref):
    tm = xn_ref.shape[0]
    bias = p_ref[0:1, :]
    neg_a = -jnp.exp(p_ref[1:2, :])
    ii = lax.broadcasted_iota(jnp.int32, (LANES, LANES), 0)
    jj = lax.broadcasted_iota(jnp.int32, (LANES, LANES), 1)
    tril = jnp.where(ii >= jj, 1.0, 0.0).astype(F32)
    ones = jnp.ones((LANES, LANES), F32)
    logits = _dot_nt(xn_ref[...], w_ref[...])
    for s in range(tm // LANES):
        rows = slice(s * LANES, (s + 1) * LANES)
        x = logits[rows, :]
        xb = x + bias
        sp = jnp.maximum(xb, 0.0) + jnp.log1p(jnp.exp(-jnp.abs(xb)))
        dec = neg_a * sp
        cum = _dot(tril, dec, HIGHEST)
        last = _dot(ones, dec, HIGHEST)
        beta = _sigmoid(x)
        ecum = jnp.exp(cum)
        etail = jnp.exp(last - cum)
        elast = jnp.exp(last)
        for f, val in enumerate((beta, cum, ecum, etail)):
            gc_ref[f, rows, :] = val
        for f, val in enumerate((beta, cum, ecum, elast)):
            gr_ref[f, s] = val.T
        for f, val in enumerate((cum, ecum)):
            mc_ref[f, rows, :] = val
        for f, val in enumerate((sp, cum, sp * etail, elast)):
            mr_ref[f, s] = val.T


def _smallprep(xn, w_small, params, tm=512):
    bt, d = xn.shape
    nb = tm // LANES

    def cspec(nf):
        return pl.BlockSpec((nf, tm, LANES), lambda i: (0, i, 0))

    def rspec(nf):
        return pl.BlockSpec((nf, nb, LANES, LANES), lambda i: (0, i, 0, 0))

    def cshape(nf):
        return jax.ShapeDtypeStruct((nf, bt, LANES), F32)

    def rshape(nf):
        return jax.ShapeDtypeStruct((nf, bt // LANES, LANES, LANES), F32)

    return pl.pallas_call(
        _smallprep_kernel,
        grid=(bt // tm,),
        in_specs=[pl.BlockSpec((tm, d), lambda i: (i, 0)),
                  pl.BlockSpec((LANES, d), lambda i: (0, 0)),
                  pl.BlockSpec((8, LANES), lambda i: (0, 0))],
        out_specs=[cspec(4), rspec(4), cspec(2), rspec(4)],
        out_shape=[cshape(4), rshape(4), cshape(2), rshape(4)],
        compiler_params=pltpu.CompilerParams(dimension_semantics=("parallel",)),
        name="smallprep",
    )(xn, w_small, params)


def _col(field, lane_iota, lane):
    return jnp.sum(jnp.where(lane_iota == lane, field, 0.0), axis=-1, keepdims=True)


def _gdn_kernel(q_ref, k_ref, v_ref, z_ref, sm_ref, smt_ref, nw_ref,
                o_ref,
                s_scr, qp_scr, op_scr, phi_scr, psi_scr, egl_scr):
    tb = pl.program_id(1)
    hg = pl.program_id(2)
    tbs = q_ref.shape[0]
    nch = tbs // GDN_CHUNK
    c = GDN_CHUNK
    dk = GDN_DK

    @pl.when(tb == 0)
    def _():
        s_scr[pl.ds(hg * GDN_HB, GDN_HB)] = jnp.zeros((GDN_HB, dk, dk), F32)

    ii = lax.broadcasted_iota(jnp.int32, (c, c), 0)
    jj = lax.broadcasted_iota(jnp.int32, (c, c), 1)
    causal = ii >= jj
    strict = ii > jj
    nlev = 4
    blk = {w: (ii // w) == (jj // w) for w in (16, 32, 64, 128) if w < c}
    ring = {w: jnp.logical_and((ii // (2 * w)) == (jj // (2 * w)), (ii // w) != (jj // w))
            for w in blk}
    lane_iota = lax.broadcasted_iota(jnp.int32, (c, LANES), 1)
    q_lane = lax.broadcasted_iota(jnp.int32, (16, c), 1)
    q_blk = q_lane // 16
    q_eye = jnp.where(q_lane % 16 == lax.broadcasted_iota(jnp.int32, (16, c), 0), 1.0, 0.0).astype(F32)
    cpi = GDN_CPI

    def phase_a(it, carry):
        st = []
        for cc in range(cpi):
            ci = it * cpi + cc
            rows = pl.ds(pl.multiple_of(ci * c, c), c)
            sm = sm_ref[:, rows, :]
            for r in range(GDN_HB):
                h = hg * GDN_HB + r
                cols = slice(r * dk, (r + 1) * dk)
                gc_r, egl_r = (smt_ref[f, ci, pl.ds(LANE_GDEC + h, 1), :] for f in (1, 3))
                st.append(dict(
                    rows=rows, ci=ci, r=r, h=h, sm=sm, qb=q_ref[rows, cols], kb=k_ref[rows, cols],
                    vb=v_ref[rows, cols], gc_r=gc_r, egl=jnp.broadcast_to(egl_r, (8, dk))))
        grams = [_dot_nt(jnp.concatenate([s_["kb"], s_["qb"]], axis=0), s_["kb"]) for s_ in st]
        for f, name, lane0 in ((0, "beta_c", LANE_BETA), (1, "gc_c", LANE_GDEC),
                               (2, "egc_c", LANE_GDEC), (3, "ekl_c", LANE_GDEC)):
            for s_ in st:
                s_[name] = _col(s_["sm"][f], lane_iota, lane0 + s_["h"])
        decs = [jnp.where(causal, jnp.exp(jnp.where(causal, s_["gc_c"] - s_["gc_r"], 0.0)), 0.0)
                for s_ in st]
        for s_, gram, dec in zip(st, grams, decs):
            s_["a"] = jnp.where(strict, gram[:c] * s_["beta_c"] * dec, 0.0)
            s_["attn"] = (gram[c:] * dec).astype(BF16)
        for s_ in st:
            s_["qg"] = s_["qb"].astype(F32) * s_["egc_c"]
            s_["kd"] = (s_["kb"].astype(F32) * s_["ekl_c"]).astype(BF16)

        widths = sorted(blk)

        def expand(q):
            return jnp.where(blk[16], jnp.concatenate([q] * (c // 16), axis=0), 0.0)

        qm, qp = [], []
        for s_ in st:
            acc = None
            for b in range(c // 16):
                term = jnp.where(q_blk == b, s_["a"][16 * b:16 * (b + 1), :], 0.0)
                acc = term if acc is None else acc + term
            qm.append(-acc)
            qp.append(q_eye)
        for lev in range(nlev):
            wm = [expand(m_) for m_ in qm]
            if lev + 1 < nlev:
                outs = [_dot3(jnp.concatenate([m_, p_], axis=0), w_) for m_, p_, w_ in zip(qm, qp, wm)]
                qm = [o_[:16] for o_ in outs]
                qp = [p_ + o_[16:] for p_, o_ in zip(qp, outs)]
            else:
                qp = [p_ + _dot3(p_, w_) for p_, w_ in zip(qp, wm)]
        tinv = [expand(p_) for p_ in qp]
        for w_blk in widths:
            nb = c // (2 * w_blk)
            eb = [jnp.where(ring[w_blk], s_["a"], 0.0).astype(BF16) for s_ in st]
            low = [slice((2 * b + 1) * w_blk, (2 * b + 2) * w_blk) for b in range(nb)]
            upp = [slice(2 * b * w_blk, (2 * b + 1) * w_blk) for b in range(nb)]
            tl = [jnp.concatenate([t_[sl] for sl in low], axis=0) for t_ in tinv]
            tb_ = [t_.astype(BF16) for t_ in tinv]
            te = [_dot(l_.astype(BF16), e_).astype(BF16) for l_, e_ in zip(tl, eb)]
            tl = [l_ - _dot(e_, b_) for l_, e_, b_ in zip(tl, te, tb_)]
            tinv = [jnp.concatenate(
                [piece for b in range(nb) for piece in (t_[upp[b]], l_[b * w_blk:(b + 1) * w_blk])],
                axis=0) for t_, l_ in zip(tinv, tl)]

        rhs = [jnp.concatenate(
            [s_["kb"].astype(F32) * (s_["beta_c"] * s_["egc_c"]), s_["vb"].astype(F32) * s_["beta_c"]],
            axis=1).astype(BF16) for s_ in st]
        wus = [_dot(t_.astype(BF16), r_).astype(BF16) for t_, r_ in zip(tinv, rhs)]
        aws = [_dot(s_["attn"], wu_) for s_, wu_ in zip(st, wus)]
        kws = [_dot_tn(s_["kd"], wu_) for s_, wu_ in zip(st, wus)]
        for s_, aw, kw in zip(st, aws, kws):
            r, rows, ci = s_["r"], s_["rows"], s_["ci"]
            qp_scr[r, rows, :] = (s_["qg"] - aw[:, :dk]).astype(BF16)
            op_scr[r, rows, :] = aw[:, dk:]
            phi_scr[r, ci] = (-kw[:, :dk]).astype(BF16)
            psi_scr[r, ci] = kw[:, dk:]
            egl_scr[r, ci] = s_["egl"]
        return carry

    lax.fori_loop(0, nch // cpi, phase_a, 0)

    nw = nw_ref[...]

    heads = range(GDN_HB)
    ss = [s_scr[hg * GDN_HB + r] for r in heads]
    for ci in range(nch):
        rows = slice(ci * c, (ci + 1) * c)
        sb = [s_.astype(BF16) for s_ in ss]
        ds = [_dot(phi_scr[r, ci], sb[r]) for r in heads]
        ss = [ss[r] * egl_scr[r, ci, 0:1, :] + ds[r] + psi_scr[r, ci] for r in heads]
        os_ = [_dot(qp_scr[r, rows, :], sb[r]) + op_scr[r, rows, :] for r in heads]
        ons = [_rms(o) * nw for o in os_]
        for r in heads:
            cols = slice(r * dk, (r + 1) * dk)
            o_ref[rows, cols] = (ons[r] * z_ref[rows, cols].astype(F32)).astype(o_ref.dtype)
    for r in heads:
        s_scr[hg * GDN_HB + r] = ss[r]


def _gdn(qk, v, z, g_fields, g_rows, norm_w, batch, seq):
    bt = qk.shape[0]
    tbs = GDN_TB
    ntb = seq // tbs
    nhg = GDN_HEADS // GDN_HB
    cw = GDN_HB * GDN_DK
    width = GDN_HEADS * GDN_DK
    row = lambda b, t, h: b * ntb + t
    return pl.pallas_call(
        _gdn_kernel,
        grid=(batch, ntb, nhg),
        in_specs=[
            pl.BlockSpec((tbs, cw), lambda b, t, h: (row(b, t, h), h)),
            pl.BlockSpec((tbs, cw), lambda b, t, h: (row(b, t, h), nhg + h)),
            pl.BlockSpec((tbs, cw), lambda b, t, h: (row(b, t, h), h)),
            pl.BlockSpec((tbs, cw), lambda b, t, h: (row(b, t, h), h)),
            pl.BlockSpec((4, tbs, LANES), lambda b, t, h: (0, row(b, t, h), 0)),
            pl.BlockSpec((4, tbs // LANES, LANES, LANES), lambda b, t, h: (0, row(b, t, h), 0, 0)),
            pl.BlockSpec((1, GDN_DK), lambda b, t, h: (0, 0)),
        ],
        out_specs=pl.BlockSpec((tbs, cw), lambda b, t, h: (row(b, t, h), h)),
        out_shape=jax.ShapeDtypeStruct((bt, width), BF16),
        scratch_shapes=[
            pltpu.VMEM((GDN_HEADS, GDN_DK, GDN_DK), F32),
            pltpu.VMEM((GDN_HB, tbs, GDN_DK), BF16),
            pltpu.VMEM((GDN_HB, tbs, GDN_DK), F32),
            pltpu.VMEM((GDN_HB, tbs // GDN_CHUNK, GDN_DK, GDN_DK), BF16),
            pltpu.VMEM((GDN_HB, tbs // GDN_CHUNK, GDN_DK, GDN_DK), F32),
            pltpu.VMEM((GDN_HB, tbs // GDN_CHUNK, 8, GDN_DK), F32),
        ],
        compiler_params=pltpu.CompilerParams(
            dimension_semantics=("arbitrary", "arbitrary", "arbitrary"),
            vmem_limit_bytes=VMEM_LIMIT),
        name="gdn",
    )(qk, qk, v, z, g_fields, g_rows, norm_w)


def _ssd_kernel(x_ref, b_ref, c_ref, z_ref, sm_ref, smt_ref, dsk_ref, nw_ref,
                o_ref,
                h_scr):
    tb = pl.program_id(1)
    g = pl.program_id(2)
    tbs = x_ref.shape[0]
    l = SSM_CHUNK
    p = SSM_P
    nch = tbs // l

    @pl.when(tb == 0)
    def _():
        h_scr[g] = jnp.zeros(h_scr.shape[1:], F32)

    ii = lax.broadcasted_iota(jnp.int32, (l, l), 0)
    jj = lax.broadcasted_iota(jnp.int32, (l, l), 1)
    causal = ii >= jj
    lane_iota = lax.broadcasted_iota(jnp.int32, (l, LANES), 1)
    dsk = dsk_ref[...]
    nw = nw_ref[...]

    first_of_pair = lane_iota < p
    npair = SSM_R // 2
    cpi = SSD_CPI

    def pick(a0, a1):
        return jnp.where(first_of_pair[:a0.shape[0]], a0, a1)

    def body(it, carry):
        h = h_scr[g]
        st = []
        for cc_ in range(cpi):
            ci = it * cpi + cc_
            rows = pl.ds(pl.multiple_of(ci * l, l), l)
            bc = b_ref[rows, :]
            cm = c_ref[rows, :]
            sm = sm_ref[:, rows, :]
            rowf = [[smt_ref[f, ci, pl.ds(LANE_DT + g * SSM_R + r, 1), :] for f in range(4)]
                    for r in range(SSM_R)]
            st.append(dict(rows=rows, xb=x_ref[rows, :], cm=cm, sm=sm, rowf=rowf,
                           scores=_dot_nt(cm, bc), bct=bc.astype(F32).T))
        for s_ in st:
            s_["ac_c"] = [_col(s_["sm"][0], lane_iota, LANE_DT + g * SSM_R + r) for r in range(SSM_R)]
            s_["eac_c"] = [_col(s_["sm"][1], lane_iota, LANE_DT + g * SSM_R + r) for r in range(SSM_R)]
        for s_ in st:
            s_["m"] = [(s_["scores"] * jnp.where(
                causal, jnp.exp(jnp.where(causal, s_["ac_c"][r] - s_["rowf"][r][1], 0.0)), 0.0)
                * s_["rowf"][r][0]).astype(BF16) for r in range(SSM_R)]
            s_["bt"] = [(s_["bct"] * s_["rowf"][r][2]).astype(BF16) for r in range(SSM_R)]
        for s_ in st:
            yd, stt, eac, eal = [], [], [], []
            for pr in range(npair):
                xp = s_["xb"][:, pr * LANES:(pr + 1) * LANES]
                r0_, r1_ = 2 * pr, 2 * pr + 1
                yd.append(pick(_dot(s_["m"][r0_], xp), _dot(s_["m"][r1_], xp)))
                stt.append(pick(_dot(s_["bt"][r0_], xp), _dot(s_["bt"][r1_], xp)))
                eac.append(pick(jnp.broadcast_to(s_["eac_c"][r0_], (l, LANES)),
                                jnp.broadcast_to(s_["eac_c"][r1_], (l, LANES))))
                eal.append(pick(s_["rowf"][r0_][3], s_["rowf"][r1_][3]))
            s_["yd"] = jnp.concatenate(yd, axis=1)
            s_["states"] = jnp.concatenate(stt, axis=1)
            s_["eac"] = jnp.concatenate(eac, axis=1)
            s_["eal"] = jnp.concatenate(eal, axis=1)
        for s_ in st:
            y_off = _dot(s_["cm"], h.astype(BF16))
            h = h * s_["eal"] + s_["states"]
            y = (s_["yd"] + y_off * s_["eac"] + dsk * s_["xb"].astype(F32)) \
                * z_ref[s_["rows"], :].astype(F32)
            o_ref[s_["rows"], :] = (_rms(y) * nw).astype(o_ref.dtype)
        h_scr[g] = h
        return carry

    lax.fori_loop(0, nch // cpi, body, 0)


def _ssd(xbc, z, m_fields, m_rows, dskip, norm_w, batch, seq, z_col0):
    bt = xbc.shape[0]
    tbs = SSD_TB
    ntb = seq // tbs
    ng = SSM_GROUPS
    gw = SSM_R * SSM_P
    inner = SSM_HEADS * SSM_P
    bb0 = inner // SSM_N
    cb0 = (inner + ng * SSM_N) // SSM_N
    zb0 = z_col0 // gw
    row = lambda b, t, g: b * ntb + t
    return pl.pallas_call(
        _ssd_kernel,
        grid=(batch, ntb, ng),
        in_specs=[
            pl.BlockSpec((tbs, gw), lambda b, t, g: (row(b, t, g), g)),
            pl.BlockSpec((tbs, SSM_N), lambda b, t, g: (row(b, t, g), bb0 + g)),
            pl.BlockSpec((tbs, SSM_N), lambda b, t, g: (row(b, t, g), cb0 + g)),
            pl.BlockSpec((tbs, gw), lambda b, t, g: (row(b, t, g), zb0 + g)),
            pl.BlockSpec((2, tbs, LANES), lambda b, t, g: (0, row(b, t, g), 0)),
            pl.BlockSpec((4, tbs // SSM_CHUNK, LANES, SSM_CHUNK),
                         lambda b, t, g: (0, row(b, t, g), 0, 0)),
            pl.BlockSpec((1, gw), lambda b, t, g: (0, g)),
            pl.BlockSpec((1, gw), lambda b, t, g: (0, g)),
        ],
        out_specs=pl.BlockSpec((tbs, gw), lambda b, t, g: (row(b, t, g), g)),
        out_shape=jax.ShapeDtypeStruct((bt, inner), BF16),
        scratch_shapes=[pltpu.VMEM((ng, SSM_N, gw), F32)],
        compiler_params=pltpu.CompilerParams(
            dimension_semantics=("arbitrary", "arbitrary", "arbitrary"),
            vmem_limit_bytes=VMEM_LIMIT),
        name="ssd",
    )(xbc, xbc, xbc, z, m_fields, m_rows, dskip, norm_w)


def _merge_kernel(o_ref, y_ref, wg_ref, ws_ref, ga_ref, gb_ref, out_ref):
    a = _dot(o_ref[...], wg_ref[...])
    b = _dot(y_ref[...], ws_ref[...])
    out_ref[...] = (ga_ref[...].astype(F32) * a + gb_ref[...].astype(F32) * b).astype(out_ref.dtype)


def _merge(o, y, wg, ws, ga, gb, tm=512, tn=1024):
    bt, d_in = o.shape
    d = wg.shape[1]
    return pl.pallas_call(
        _merge_kernel,
        grid=(d // tn, bt // tm),
        in_specs=[
            pl.BlockSpec((tm, d_in), lambda j, i: (i, 0)),
            pl.BlockSpec((tm, d_in), lambda j, i: (i, 0)),
            pl.BlockSpec((d_in, tn), lambda j, i: (0, j)),
            pl.BlockSpec((d_in, tn), lambda j, i: (0, j)),
            pl.BlockSpec((tm, tn), lambda j, i: (i, j)),
            pl.BlockSpec((tm, tn), lambda j, i: (i, j)),
        ],
        out_specs=pl.BlockSpec((tm, tn), lambda j, i: (i, j)),
        out_shape=jax.ShapeDtypeStruct((bt, d), BF16),
        compiler_params=pltpu.CompilerParams(
            dimension_semantics=("parallel", "parallel"),
            vmem_limit_bytes=VMEM_LIMIT),
        name="merge",
    )(o, y, wg, ws, ga, gb)


def _oproj_kernel(m_ref, w_ref, h_ref, out_ref):
    out_ref[...] = h_ref[...] + _dot(m_ref[...], w_ref[...])


def _oproj(merged, wo, h, tm=512, tn=1024):
    bt, d_in = merged.shape
    d = wo.shape[1]
    return pl.pallas_call(
        _oproj_kernel,
        grid=(d // tn, bt // tm),
        in_specs=[
            pl.BlockSpec((tm, d_in), lambda j, i: (i, 0)),
            pl.BlockSpec((d_in, tn), lambda j, i: (0, j)),
            pl.BlockSpec((tm, tn), lambda j, i: (i, j)),
        ],
        out_specs=pl.BlockSpec((tm, tn), lambda j, i: (i, j)),
        out_shape=jax.ShapeDtypeStruct((bt, d), F32),
        compiler_params=pltpu.CompilerParams(
            dimension_semantics=("parallel", "parallel"),
            vmem_limit_bytes=VMEM_LIMIT),
        name="oproj",
    )(merged, wo, h)


def _mlp_kernel(h_ref, nw_ref, wu_ref, wd_ref, fw_ref, out_ref, xn_scr, acc_scr, *, final):
    k = pl.program_id(1)

    @pl.when(k == 0)
    def _():
        xn_scr[...] = (_rms(h_ref[...]) * nw_ref[...]).astype(BF16)
        acc_scr[...] = jnp.zeros_like(acc_scr)

    up = _dot(xn_scr[...], wu_ref[...])
    act = jnp.square(jnp.maximum(up, 0.0)).astype(BF16)
    acc_scr[...] += _dot(act, wd_ref[...])

    @pl.when(k == pl.num_programs(1) - 1)
    def _():
        y = h_ref[...] + acc_scr[...]
        if final:
            y = _rms(y) * fw_ref[...]
        out_ref[...] = y


def _mlp(h, norm_w, w_up, w_down, final_w, final, tm=512, th=1024):
    bt, d = h.shape
    hid = w_up.shape[1]
    return pl.pallas_call(
        functools.partial(_mlp_kernel, final=final),
        grid=(bt // tm, hid // th),
        in_specs=[
            pl.BlockSpec((tm, d), lambda i, k: (i, 0)),
            pl.BlockSpec((1, d), lambda i, k: (0, 0)),
            pl.BlockSpec((d, th), lambda i, k: (0, k)),
            pl.BlockSpec((th, d), lambda i, k: (k, 0)),
            pl.BlockSpec((1, d), lambda i, k: (0, 0)),
        ],
        out_specs=pl.BlockSpec((tm, d), lambda i, k: (i, 0)),
        out_shape=jax.ShapeDtypeStruct((bt, d), F32),
        scratch_shapes=[pltpu.VMEM((tm, d), BF16), pltpu.VMEM((tm, d), F32)],
        compiler_params=pltpu.CompilerParams(
            dimension_semantics=("parallel", "arbitrary"),
            vmem_limit_bytes=VMEM_LIMIT),
        name="mlp",
    )(h, norm_w, w_up, w_down, final_w)


def kernel(x, norm1_w, w_in, gdn_conv_w, gdn_a_log, gdn_dt_bias, gdn_norm_w, ssm_conv_w, ssm_conv_b, ssm_a_log, ssm_dt_bias, ssm_d, ssm_norm_w, w_gdn_out, w_ssm_out, w_o, norm2_w, w_up, w_down, final_norm_w):
    batch, seq, d = x.shape
    depth = w_in.shape[0]
    assert depth >= 1
    bt = batch * seq
    gw = GDN_HEADS * GDN_DK
    inner = SSM_HEADS * SSM_P
    conv_ch = inner + 2 * SSM_GROUPS * SSM_N
    sizes = (3 * gw, gw, GDN_HEADS, GDN_HEADS, inner, conv_ch, SSM_HEADS, d, d)
    offs = [0]
    for s in sizes:
        offs.append(offs[-1] + s)
    o_qkv, o_gz, o_gb, o_ga, o_sz, o_xbc, o_dt, o_gate_a, o_gate_b, _ = offs
    no_w = jnp.zeros((CONV_K, max(gw, inner, d)), F32)
    no_b = jnp.zeros((1, max(2 * gw, conv_ch, d)), F32)
    zpad = jnp.zeros((LANES - 2 * GDN_HEADS - SSM_HEADS,), F32)

    h = x.reshape(bt, d)
    for l in range(depth):
        wt = jnp.swapaxes(w_in[l], 0, 1)

        def seg(a, n):
            return wt[a:a + n].astype(BF16)

        w_qk, w_v, w_xbc = seg(o_qkv, 2 * gw), seg(o_qkv + 2 * gw, gw), seg(o_xbc, conv_ch)
        w_gz, w_sz, w_ga, w_gb = seg(o_gz, gw), seg(o_sz, inner), seg(o_gate_a, d), seg(o_gate_b, d)
        w_small = jnp.concatenate(
            [wt[o_gb:o_gb + GDN_HEADS], wt[o_ga:o_ga + GDN_HEADS], wt[o_dt:o_dt + SSM_HEADS],
             jnp.zeros((LANES - 2 * GDN_HEADS - SSM_HEADS, d), wt.dtype)], axis=0).astype(BF16)

        xn = _norm(h, norm1_w[l][None, :])
        proj = functools.partial(_proj, xn, seq=seq)
        qk = proj(w_qk, gdn_conv_w[l][:, :2 * gw], no_b, conv=True, act="silu",
                  l2norm=True, q_tiles=gw // PROJ_TN, name="proj_qk")
        v = proj(w_v, gdn_conv_w[l][:, 2 * gw:], no_b, conv=True, act="silu",
                 l2norm=False, q_tiles=0, name="proj_v")
        xbc = proj(w_xbc, ssm_conv_w[l], ssm_conv_b[l][None, :], conv=True, act="silu",
                   l2norm=False, q_tiles=0, name="proj_xbc")
        gz = proj(w_gz, no_w, no_b, conv=False, act="silu", l2norm=False, q_tiles=0, name="proj_gz")
        sz = proj(w_sz, no_w, no_b, conv=False, act="silu", l2norm=False, q_tiles=0, name="proj_sz")
        ga = proj(w_ga, no_w, no_b, conv=False, act="sigmoid", l2norm=False, q_tiles=0,
                  name="proj_ga")
        gb = proj(w_gb, no_w, no_b, conv=False, act="sigmoid", l2norm=False, q_tiles=0,
                  name="proj_gb")

        params = jnp.zeros((8, LANES), F32)
        params = params.at[0].set(jnp.concatenate(
            [jnp.zeros((GDN_HEADS,), F32), gdn_dt_bias[l], ssm_dt_bias[l], zpad]))
        params = params.at[1].set(jnp.concatenate(
            [jnp.zeros((GDN_HEADS,), F32), gdn_a_log[l], ssm_a_log[l], zpad]))
        g_cols, g_rows, m_cols, m_rows = _smallprep(xn, w_small, params)

        o = _gdn(qk, v, gz, g_cols, g_rows, gdn_norm_w[l][None, :], batch, seq)
        dskip = jnp.repeat(ssm_d[l], SSM_P)[None, :]
        y = _ssd(xbc, sz, m_cols, m_rows, dskip, ssm_norm_w[l][None, :], batch, seq, 0)
        merged = _merge(o, y, w_gdn_out[l].astype(BF16), w_ssm_out[l].astype(BF16), ga, gb)
        h = _oproj(merged, w_o[l].astype(BF16), h)
        h = _mlp(h, norm2_w[l][None, :], w_up[l].astype(BF16), w_down[l].astype(BF16),
                 final_norm_w[None, :], final=(l == depth - 1))
    return h.reshape(batch, seq, d)
```

```python
import functools

import jax
import jax.numpy as jnp
from jax import lax
from jax.experimental import pallas as pl
from jax.experimental.pallas import tpu as pltpu

F32 = jnp.float32
BF16 = jnp.bfloat16
HIGHEST = lax.Precision.HIGHEST
EPS = 1e-6

LANES = 128
CONV_K = 4
GDN_HEADS = 16
GDN_DK = 128
GDN_CHUNK = 128
GDN_HB = 4
GDN_CPI = 4
SSM_HEADS = 32
SSM_P = 64
SSM_GROUPS = 8
SSM_N = 128
SSM_CHUNK = 128
SSM_R = SSM_HEADS // SSM_GROUPS
SSD_CPI = 4
GDN_TB = 512
SSD_TB = 1024
PROJ_TN = 2048
PROJ_SUB = 256
PROJ_ROWS = 64
PROJ_MSPLIT = 2
VMEM_LIMIT = 48 * 1024 * 1024

LANE_BETA = 0
LANE_GDEC = 16
LANE_DT = 32


def _sigmoid(x):
    return 1.0 / (1.0 + jnp.exp(-x))


def _silu(x):
    return x * _sigmoid(x)


def _dot(a, b, precision=None):
    return jnp.dot(a, b, preferred_element_type=F32, precision=precision)


def _dot3(a, b):
    ah = a.astype(BF16)
    al = (a - ah.astype(F32)).astype(BF16)
    bh = b.astype(BF16)
    bl = (b - bh.astype(F32)).astype(BF16)
    return _dot(ah, bh) + (_dot(ah, bl) + _dot(al, bh))


def _dot_nt(a, b):
    return lax.dot_general(a, b, (((1,), (1,)), ((), ())), preferred_element_type=F32)


def _dot_tn(a, b):
    return lax.dot_general(a, b, (((0,), (0,)), ((), ())), preferred_element_type=F32)


def _rms(x):
    return x * lax.rsqrt(jnp.mean(x * x, axis=-1, keepdims=True) + EPS)


def _norm_kernel(x_ref, w_ref, o_ref):
    o_ref[...] = (_rms(x_ref[...]) * w_ref[...]).astype(o_ref.dtype)


def _norm(x2, w, tm=512):
    bt, d = x2.shape
    return pl.pallas_call(
        _norm_kernel,
        grid=(bt // tm,),
        in_specs=[pl.BlockSpec((tm, d), lambda i: (i, 0)), pl.BlockSpec((1, d), lambda i: (0, 0))],
        out_specs=pl.BlockSpec((tm, d), lambda i: (i, 0)),
        out_shape=jax.ShapeDtypeStruct((bt, d), BF16),
        compiler_params=pltpu.CompilerParams(dimension_semantics=("parallel",)),
        name="norm",
    )(x2, w)


def _wprep_kernel(w_ref, *o_refs, segs, smalls):
    for (a, n), o_ref in zip(segs, o_refs[:-1]):
        o_ref[...] = w_ref[a:a + n, :].astype(o_ref.dtype)
    parts = [w_ref[a:a + n, :] for a, n in smalls]
    parts.append(jnp.zeros((LANES - sum(n for _, n in smalls), w_ref.shape[1]), F32))
    o_refs[-1][...] = jnp.concatenate(parts, axis=0).astype(o_refs[-1].dtype)


def _wprep(wt, segs, smalls, tc=LANES):
    n_all, d = wt.shape
    heights = [n for _, n in segs] + [LANES]
    return pl.pallas_call(
        functools.partial(_wprep_kernel, segs=segs, smalls=smalls),
        grid=(d // tc,),
        in_specs=[pl.BlockSpec((n_all, tc), lambda i: (0, i))],
        out_specs=[pl.BlockSpec((n, tc), lambda i: (0, i)) for n in heights],
        out_shape=[jax.ShapeDtypeStruct((n, d), BF16) for n in heights],
        compiler_params=pltpu.CompilerParams(
            dimension_semantics=("parallel",), vmem_limit_bytes=VMEM_LIMIT),
        name="wprep",
    )(wt)


def _proj_kernel(xn_ref, w_ref, cw_ref, cb_ref, o_ref, *acc_scrs, conv, act, l2norm, q_tiles, seq):
    j = pl.program_id(0)
    i = pl.program_id(1)
    tm, tn = o_ref.shape
    nsub = tn // PROJ_SUB
    if conv:
        @pl.when((i * tm) % seq == 0)
        def _():
            for s in range(nsub):
                acc_scrs[s][0:8, :] = jnp.zeros((8, PROJ_SUB), F32)
    if l2norm:
        scale = jnp.where(j < q_tiles, GDN_DK ** -0.5, 1.0)
    z0 = pl.multiple_of(jnp.minimum(i, 0), 8)

    mh = tm // PROJ_MSPLIT

    def matmul(s, m):
        acc_scrs[s][8 + m * mh:8 + (m + 1) * mh, :] = _dot_nt(
            xn_ref[m * mh:(m + 1) * mh, :], w_ref[s * PROJ_SUB:(s + 1) * PROJ_SUB, :])

    def epilogue(s, m):
        acc_scr = acc_scrs[s]
        pieces = [(rc * PROJ_ROWS, lc * LANES)
                  for rc in range(m * mh // PROJ_ROWS, (m + 1) * mh // PROJ_ROWS)
                  for lc in range(PROJ_SUB // LANES)]
        for r0, l0 in pieces:
            sub = slice(l0, l0 + LANES)
            cols = slice(s * PROJ_SUB + l0, s * PROJ_SUB + l0 + LANES)
            if conv:
                win = acc_scr[pl.ds(z0 + r0, PROJ_ROWS + 8), sub]
                prev = pltpu.roll(win, 1, axis=0)
                lo = win * cw_ref[1:2, cols] + prev * cw_ref[0:1, cols]
                y = (win * cw_ref[3:4, cols] + prev * cw_ref[2:3, cols]
                     + pltpu.roll(lo, 2, axis=0))[8:] + cb_ref[:, cols]
            else:
                y = acc_scr[pl.ds(z0 + 8 + r0, PROJ_ROWS), sub]
            y = _silu(y) if act == "silu" else _sigmoid(y)
            if l2norm:
                y = y * (lax.rsqrt(jnp.sum(y * y, axis=-1, keepdims=True) + EPS) * scale)
            o_ref[r0:r0 + PROJ_ROWS, cols] = y.astype(o_ref.dtype)
        if conv and m == PROJ_MSPLIT - 1:
            acc_scr[0:8, :] = acc_scr[tm:tm + 8, :]

    units = [(s, m) for s in range(nsub) for m in range(PROJ_MSPLIT)]
    ahead = 2
    for u in units[:ahead]:
        matmul(*u)
    for n_, u in enumerate(units):
        if n_ + ahead < len(units):
            matmul(*units[n_ + ahead])
        epilogue(*u)


def _proj(xn, wt, cw, cb, *, conv, act, l2norm, q_tiles, seq, name, tm=1024, tn=PROJ_TN):
    bt, d = xn.shape
    n = wt.shape[0]
    assert seq % tm == 0 and n % tn == 0
    body = functools.partial(_proj_kernel, conv=conv, act=act, l2norm=l2norm, q_tiles=q_tiles, seq=seq)
    return pl.pallas_call(
        body,
        grid=(n // tn, bt // tm),
        in_specs=[
            pl.BlockSpec((tm, d), lambda j, i: (i, 0)),
            pl.BlockSpec((tn, d), lambda j, i: (j, 0)),
            pl.BlockSpec((CONV_K, tn), lambda j, i: (0, j)),
            pl.BlockSpec((1, tn), lambda j, i: (0, j)),
        ],
        out_specs=pl.BlockSpec((tm, tn), lambda j, i: (i, j)),
        out_shape=jax.ShapeDtypeStruct((bt, n), BF16),
        scratch_shapes=[pltpu.VMEM((tm + 8, PROJ_SUB), F32) for _ in range(tn // PROJ_SUB)],
        compiler_params=pltpu.CompilerParams(
            dimension_semantics=("arbitrary", "arbitrary"),
            vmem_limit_bytes=VMEM_LIMIT),
        name=name,
    )(xn, wt, cw, cb)


def _smallprep_kernel(xn_ref, w_ref, p_ref, gc_ref, gr_ref, mc_ref, mr_ref):
    tm = xn_ref.shape[0]
    bias = p_ref[0:1, :]
    neg_a = -jnp.exp(p_ref[1:2, :])
    ii = lax.broadcasted_iota(jnp.int32, (LANES, LANES), 0)
    jj = lax.broadcasted_iota(jnp.int32, (LANES, LANES), 1)
    tril = jnp.where(ii >= jj, 1.0, 0.0).astype(F32)
    ones = jnp.ones((LANES, LANES), F32)
    logits = _dot_nt(xn_ref[...], w_ref[...])
    for s in range(tm // LANES):
        rows = slice(s * LANES, (s + 1) * LANES)
        x = logits[rows, :]
        xb = x + bias
        sp = jnp.maximum(xb, 0.0) + jnp.log1p(jnp.exp(-jnp.abs(xb)))
        dec = neg_a * sp
        cum = _dot(tril, dec, HIGHEST)
        last = _dot(ones, dec, HIGHEST)
        beta = _sigmoid(x)
        ecum = jnp.exp(cum)
        etail = jnp.exp(last - cum)
        elast = jnp.exp(last)
        for f, val in enumerate((beta, cum, ecum, etail)):
            gc_ref[f, rows, :] = val
        for f, val in enumerate((beta, cum, ecum, elast)):
            gr_ref[f, s] = val.T
        for f, val in enumerate((cum, ecum)):
            mc_ref[f, rows, :] = val
        for f, val in enumerate((sp, cum, sp * etail, elast)):
            mr_ref[f, s] = val.T


def _smallprep(xn, w_small, params, tm=512):
    bt, d = xn.shape
    nb = tm // LANES

    def cspec(nf):
        return pl.BlockSpec((nf, tm, LANES), lambda i: (0, i, 0))

    def rspec(nf):
        return pl.BlockSpec((nf, nb, LANES, LANES), lambda i: (0, i, 0, 0))

    def cshape(nf):
        return jax.ShapeDtypeStruct((nf, bt, LANES), F32)

    def rshape(nf):
        return jax.ShapeDtypeStruct((nf, bt // LANES, LANES, LANES), F32)

    return pl.pallas_call(
        _smallprep_kernel,
        grid=(bt // tm,),
        in_specs=[pl.BlockSpec((tm, d), lambda i: (i, 0)),
                  pl.BlockSpec((LANES, d), lambda i: (0, 0)),
                  pl.BlockSpec((8, LANES), lambda i: (0, 0))],
        out_specs=[cspec(4), rspec(4), cspec(2), rspec(4)],
        out_shape=[cshape(4), rshape(4), cshape(2), rshape(4)],
        compiler_params=pltpu.CompilerParams(dimension_semantics=("parallel",)),
        name="smallprep",
    )(xn, w_small, params)


def _col(field, lane_iota, lane):
    return jnp.sum(jnp.where(lane_iota == lane, field, 0.0), axis=-1, keepdims=True)


def _gdn_kernel(q_ref, k_ref, v_ref, z_ref, sm_ref, smt_ref, nw_ref,
                o_ref,
                s_scr, qp_scr, op_scr, phi_scr, psi_scr, egl_scr):
    tb = pl.program_id(1)
    hg = pl.program_id(2)
    tbs = q_ref.shape[0]
    nch = tbs // GDN_CHUNK
    c = GDN_CHUNK
    dk = GDN_DK

    @pl.when(tb == 0)
    def _():
        s_scr[pl.ds(hg * GDN_HB, GDN_HB)] = jnp.zeros((GDN_HB, dk, dk), F32)

    ii = lax.broadcasted_iota(jnp.int32, (c, c), 0)
    jj = lax.broadcasted_iota(jnp.int32, (c, c), 1)
    causal = ii >= jj
    strict = ii > jj
    nlev = 4
    blk = {w: (ii // w) == (jj // w) for w in (16, 32, 64, 128) if w < c}
    ring = {w: jnp.logical_and((ii // (2 * w)) == (jj // (2 * w)), (ii // w) != (jj // w))
            for w in blk}
    lane_iota = lax.broadcasted_iota(jnp.int32, (c, LANES), 1)
    q_lane = lax.broadcasted_iota(jnp.int32, (16, c), 1)
    q_blk = q_lane // 16
    q_eye = jnp.where(q_lane % 16 == lax.broadcasted_iota(jnp.int32, (16, c), 0), 1.0, 0.0).astype(F32)
    cpi = GDN_CPI

    def phase_a(it, carry):
        st = []
        for cc in range(cpi):
            ci = it * cpi + cc
            rows = pl.ds(pl.multiple_of(ci * c, c), c)
            sm = sm_ref[:, rows, :]
            for r in range(GDN_HB):
                h = hg * GDN_HB + r
                cols = slice(r * dk, (r + 1) * dk)
                gc_r, egl_r = (smt_ref[f, ci, pl.ds(LANE_GDEC + h, 1), :] for f in (1, 3))
                st.append(dict(
                    rows=rows, ci=ci, r=r, h=h, sm=sm, qb=q_ref[rows, cols], kb=k_ref[rows, cols],
                    vb=v_ref[rows, cols], gc_r=gc_r, egl=jnp.broadcast_to(egl_r, (8, dk))))
        grams = [_dot_nt(jnp.concatenate([s_["kb"], s_["qb"]], axis=0), s_["kb"]) for s_ in st]
        for f, name, lane0 in ((0, "beta_c", LANE_BETA), (1, "gc_c", LANE_GDEC),
                               (2, "egc_c", LANE_GDEC), (3, "ekl_c", LANE_GDEC)):
            for s_ in st:
                s_[name] = _col(s_["sm"][f], lane_iota, lane0 + s_["h"])
        decs = [jnp.where(causal, jnp.exp(jnp.where(causal, s_["gc_c"] - s_["gc_r"], 0.0)), 0.0)
                for s_ in st]
        for s_, gram, dec in zip(st, grams, decs):
            s_["a"] = jnp.where(strict, gram[:c] * s_["beta_c"] * dec, 0.0)
            s_["attn"] = (gram[c:] * dec).astype(BF16)
        for s_ in st:
            s_["qg"] = s_["qb"].astype(F32) * s_["egc_c"]
            s_["kd"] = (s_["kb"].astype(F32) * s_["ekl_c"]).astype(BF16)

        widths = sorted(blk)

        def expand(q):
            return jnp.where(blk[16], jnp.concatenate([q] * (c // 16), axis=0), 0.0)

        qm, qp = [], []
        for s_ in st:
            acc = None
            for b in range(c // 16):
                term = jnp.where(q_blk == b, s_["a"][16 * b:16 * (b + 1), :], 0.0)
                acc = term if acc is None else acc + term
            qm.append(-acc)
            qp.append(q_eye)
        for lev in range(nlev):
            wm = [expand(m_) for m_ in qm]
            if lev + 1 < nlev:
                outs = [_dot3(jnp.concatenate([m_, p_], axis=0), w_) for m_, p_, w_ in zip(qm, qp, wm)]
                qm = [o_[:16] for o_ in outs]
                qp = [p_ + o_[16:] for p_, o_ in zip(qp, outs)]
            else:
                qp = [p_ + _dot3(p_, w_) for p_, w_ in zip(qp, wm)]
        tinv = [expand(p_) for p_ in qp]
        for w_blk in widths:
            nb = c // (2 * w_blk)
            eb = [jnp.where(ring[w_blk], s_["a"], 0.0).astype(BF16) for s_ in st]
            low = [slice((2 * b + 1) * w_blk, (2 * b + 2) * w_blk) for b in range(nb)]
            upp = [slice(2 * b * w_blk, (2 * b + 1) * w_blk) for b in range(nb)]
            tl = [jnp.concatenate([t_[sl] for sl in low], axis=0) for t_ in tinv]
            tb_ = [t_.astype(BF16) for t_ in tinv]
            te = [_dot(l_.astype(BF16), e_).astype(BF16) for l_, e_ in zip(tl, eb)]
            tl = [l_ - _dot(e_, b_) for l_, e_, b_ in zip(tl, te, tb_)]
            tinv = [jnp.concatenate(
                [piece for b in range(nb) for piece in (t_[upp[b]], l_[b * w_blk:(b + 1) * w_blk])],
                axis=0) for t_, l_ in zip(tinv, tl)]

        rhs = [jnp.concatenate(
            [s_["kb"].astype(F32) * (s_["beta_c"] * s_["egc_c"]), s_["vb"].astype(F32) * s_["beta_c"]],
            axis=1).astype(BF16) for s_ in st]
        wus = [_dot(t_.astype(BF16), r_).astype(BF16) for t_, r_ in zip(tinv, rhs)]
        aws = [_dot(s_["attn"], wu_) for s_, wu_ in zip(st, wus)]
        kws = [_dot_tn(s_["kd"], wu_) for s_, wu_ in zip(st, wus)]
        for s_, aw, kw in zip(st, aws, kws):
            r, rows, ci = s_["r"], s_["rows"], s_["ci"]
            qp_scr[r, rows, :] = (s_["qg"] - aw[:, :dk]).astype(BF16)
            op_scr[r, rows, :] = aw[:, dk:]
            phi_scr[r, ci] = (-kw[:, :dk]).astype(BF16)
            psi_scr[r, ci] = kw[:, dk:]
            egl_scr[r, ci] = s_["egl"]
        return carry

    lax.fori_loop(0, nch // cpi, phase_a, 0)

    nw = nw_ref[...]

    heads = range(GDN_HB)
    ss = [s_scr[hg * GDN_HB + r] for r in heads]
    for ci in range(nch):
        rows = slice(ci * c, (ci + 1) * c)
        sb = [s_.astype(BF16) for s_ in ss]
        ds = [_dot(phi_scr[r, ci], sb[r]) for r in heads]
        ss = [ss[r] * egl_scr[r, ci, 0:1, :] + ds[r] + psi_scr[r, ci] for r in heads]
        os_ = [_dot(qp_scr[r, rows, :], sb[r]) + op_scr[r, rows, :] for r in heads]
        ons = [_rms(o) * nw for o in os_]
        for r in heads:
            cols = slice(r * dk, (r + 1) * dk)
            o_ref[rows, cols] = (ons[r] * z_ref[rows, cols].astype(F32)).astype(o_ref.dtype)
    for r in heads:
        s_scr[hg * GDN_HB + r] = ss[r]


def _gdn(qk, v, z, g_fields, g_rows, norm_w, batch, seq):
    bt = qk.shape[0]
    tbs = GDN_TB
    ntb = seq // tbs
    nhg = GDN_HEADS // GDN_HB
    cw = GDN_HB * GDN_DK
    width = GDN_HEADS * GDN_DK
    row = lambda b, t, h: b * ntb + t
    return pl.pallas_call(
        _gdn_kernel,
        grid=(batch, ntb, nhg),
        in_specs=[
            pl.BlockSpec((tbs, cw), lambda b, t, h: (row(b, t, h), h)),
            pl.BlockSpec((tbs, cw), lambda b, t, h: (row(b, t, h), nhg + h)),
            pl.BlockSpec((tbs, cw), lambda b, t, h: (row(b, t, h), h)),
            pl.BlockSpec((tbs, cw), lambda b, t, h: (row(b, t, h), h)),
            pl.BlockSpec((4, tbs, LANES), lambda b, t, h: (0, row(b, t, h), 0)),
            pl.BlockSpec((4, tbs // LANES, LANES, LANES), lambda b, t, h: (0, row(b, t, h), 0, 0)),
            pl.BlockSpec((1, GDN_DK), lambda b, t, h: (0, 0)),
        ],
        out_specs=pl.BlockSpec((tbs, cw), lambda b, t, h: (row(b, t, h), h)),
        out_shape=jax.ShapeDtypeStruct((bt, width), BF16),
        scratch_shapes=[
            pltpu.VMEM((GDN_HEADS, GDN_DK, GDN_DK), F32),
            pltpu.VMEM((GDN_HB, tbs, GDN_DK), BF16),
            pltpu.VMEM((GDN_HB, tbs, GDN_DK), F32),
            pltpu.VMEM((GDN_HB, tbs // GDN_CHUNK, GDN_DK, GDN_DK), BF16),
            pltpu.VMEM((GDN_HB, tbs // GDN_CHUNK, GDN_DK, GDN_DK), F32),
            pltpu.VMEM((GDN_HB, tbs // GDN_CHUNK, 8, GDN_DK), F32),
        ],
        compiler_params=pltpu.CompilerParams(
            dimension_semantics=("arbitrary", "arbitrary", "arbitrary"),
            vmem_limit_bytes=VMEM_LIMIT),
        name="gdn",
    )(qk, qk, v, z, g_fields, g_rows, norm_w)


def _ssd_kernel(x_ref, b_ref, c_ref, z_ref, sm_ref, smt_ref, dsk_ref, nw_ref,
                o_ref,
                h_scr):
    tb = pl.program_id(1)
    g = pl.program_id(2)
    tbs = x_ref.shape[0]
    l = SSM_CHUNK
    p = SSM_P
    nch = tbs // l

    @pl.when(tb == 0)
    def _():
        h_scr[g] = jnp.zeros(h_scr.shape[1:], F32)

    ii = lax.broadcasted_iota(jnp.int32, (l, l), 0)
    jj = lax.broadcasted_iota(jnp.int32, (l, l), 1)
    causal = ii >= jj
    lane_iota = lax.broadcasted_iota(jnp.int32, (l, LANES), 1)
    dsk = dsk_ref[...]
    nw = nw_ref[...]

    first_of_pair = lane_iota < p
    npair = SSM_R // 2
    cpi = SSD_CPI

    def pick(a0, a1):
        return jnp.where(first_of_pair[:a0.shape[0]], a0, a1)

    def body(it, carry):
        h = h_scr[g]
        st = []
        for cc_ in range(cpi):
            ci = it * cpi + cc_
            rows = pl.ds(pl.multiple_of(ci * l, l), l)
            bc = b_ref[rows, :]
            cm = c_ref[rows, :]
            sm = sm_ref[:, rows, :]
            rowf = [[smt_ref[f, ci, pl.ds(LANE_DT + g * SSM_R + r, 1), :] for f in range(4)]
                    for r in range(SSM_R)]
            st.append(dict(rows=rows, xb=x_ref[rows, :], cm=cm, sm=sm, rowf=rowf,
                           scores=_dot_nt(cm, bc), bct=bc.astype(F32).T))
        for s_ in st:
            s_["ac_c"] = [_col(s_["sm"][0], lane_iota, LANE_DT + g * SSM_R + r) for r in range(SSM_R)]
            s_["eac_c"] = [_col(s_["sm"][1], lane_iota, LANE_DT + g * SSM_R + r) for r in range(SSM_R)]
        for s_ in st:
            s_["m"] = [(s_["scores"] * jnp.where(
                causal, jnp.exp(jnp.where(causal, s_["ac_c"][r] - s_["rowf"][r][1], 0.0)), 0.0)
                * s_["rowf"][r][0]).astype(BF16) for r in range(SSM_R)]
            s_["bt"] = [(s_["bct"] * s_["rowf"][r][2]).astype(BF16) for r in range(SSM_R)]
        for s_ in st:
            yd, stt, eac, eal = [], [], [], []
            for pr in range(npair):
                xp = s_["xb"][:, pr * LANES:(pr + 1) * LANES]
                r0_, r1_ = 2 * pr, 2 * pr + 1
                yd.append(pick(_dot(s_["m"][r0_], xp), _dot(s_["m"][r1_], xp)))
                stt.append(pick(_dot(s_["bt"][r0_], xp), _dot(s_["bt"][r1_], xp)))
                eac.append(pick(jnp.broadcast_to(s_["eac_c"][r0_], (l, LANES)),
                                jnp.broadcast_to(s_["eac_c"][r1_], (l, LANES))))
                eal.append(pick(s_["rowf"][r0_][3], s_["rowf"][r1_][3]))
            s_["yd"] = jnp.concatenate(yd, axis=1)
            s_["states"] = jnp.concatenate(stt, axis=1)
            s_["eac"] = jnp.concatenate(eac, axis=1)
            s_["eal"] = jnp.concatenate(eal, axis=1)
        for s_ in st:
            y_off = _dot(s_["cm"], h.astype(BF16))
            h = h * s_["eal"] + s_["states"]
            y = (s_["yd"] + y_off * s_["eac"] + dsk * s_["xb"].astype(F32)) \
                * z_ref[s_["rows"], :].astype(F32)
            o_ref[s_["rows"], :] = (_rms(y) * nw).astype(o_ref.dtype)
        h_scr[g] = h
        return carry

    lax.fori_loop(0, nch // cpi, body, 0)


def _ssd(xbc, z, m_fields, m_rows, dskip, norm_w, batch, seq, z_col0):
    bt = xbc.shape[0]
    tbs = SSD_TB
    ntb = seq // tbs
    ng = SSM_GROUPS
    gw = SSM_R * SSM_P
    inner = SSM_HEADS * SSM_P
    bb0 = inner // SSM_N
    cb0 = (inner + ng * SSM_N) // SSM_N
    zb0 = z_col0 // gw
    row = lambda b, t, g: b * ntb + t
    return pl.pallas_call(
        _ssd_kernel,
        grid=(batch, ntb, ng),
        in_specs=[
            pl.BlockSpec((tbs, gw), lambda b, t, g: (row(b, t, g), g)),
            pl.BlockSpec((tbs, SSM_N), lambda b, t, g: (row(b, t, g), bb0 + g)),
            pl.BlockSpec((tbs, SSM_N), lambda b, t, g: (row(b, t, g), cb0 + g)),
            pl.BlockSpec((tbs, gw), lambda b, t, g: (row(b, t, g), zb0 + g)),
            pl.BlockSpec((2, tbs, LANES), lambda b, t, g: (0, row(b, t, g), 0)),
            pl.BlockSpec((4, tbs // SSM_CHUNK, LANES, SSM_CHUNK),
                         lambda b, t, g: (0, row(b, t, g), 0, 0)),
            pl.BlockSpec((1, gw), lambda b, t, g: (0, g)),
            pl.BlockSpec((1, gw), lambda b, t, g: (0, g)),
        ],
        out_specs=pl.BlockSpec((tbs, gw), lambda b, t, g: (row(b, t, g), g)),
        out_shape=jax.ShapeDtypeStruct((bt, inner), BF16),
        scratch_shapes=[pltpu.VMEM((ng, SSM_N, gw), F32)],
        compiler_params=pltpu.CompilerParams(
            dimension_semantics=("arbitrary", "arbitrary", "arbitrary"),
            vmem_limit_bytes=VMEM_LIMIT),
        name="ssd",
    )(xbc, xbc, xbc, z, m_fields, m_rows, dskip, norm_w)


def _merge_kernel(o_ref, y_ref, wg_ref, ws_ref, ga_ref, gb_ref, out_ref):
    a = _dot(o_ref[...], wg_ref[...])
    b = _dot(y_ref[...], ws_ref[...])
    out_ref[...] = (ga_ref[...].astype(F32) * a + gb_ref[...].astype(F32) * b).astype(out_ref.dtype)


def _merge(o, y, wg, ws, ga, gb, tm=512, tn=1024):
    bt, d_in = o.shape
    d = wg.shape[1]
    return pl.pallas_call(
        _merge_kernel,
        grid=(d // tn, bt // tm),
        in_specs=[
            pl.BlockSpec((tm, d_in), lambda j, i: (i, 0)),
            pl.BlockSpec((tm, d_in), lambda j, i: (i, 0)),
            pl.BlockSpec((d_in, tn), lambda j, i: (0, j)),
            pl.BlockSpec((d_in, tn), lambda j, i: (0, j)),
            pl.BlockSpec((tm, tn), lambda j, i: (i, j)),
            pl.BlockSpec((tm, tn), lambda j, i: (i, j)),
        ],
        out_specs=pl.BlockSpec((tm, tn), lambda j, i: (i, j)),
        out_shape=jax.ShapeDtypeStruct((bt, d), BF16),
        compiler_params=pltpu.CompilerParams(
            dimension_semantics=("parallel", "parallel"),
            vmem_limit_bytes=VMEM_LIMIT),
        name="merge",
    )(o, y, wg, ws, ga, gb)


def _oproj_kernel(m_ref, w_ref, h_ref, out_ref):
    out_ref[...] = h_ref[...] + _dot(m_ref[...], w_ref[...])


def _oproj(merged, wo, h, tm=512, tn=1024):
    bt, d_in = merged.shape
    d = wo.shape[1]
    return pl.pallas_call(
        _oproj_kernel,
        grid=(d // tn, bt // tm),
        in_specs=[
            pl.BlockSpec((tm, d_in), lambda j, i: (i, 0)),
            pl.BlockSpec((d_in, tn), lambda j, i: (0, j)),
            pl.BlockSpec((tm, tn), lambda j, i: (i, j)),
        ],
        out_specs=pl.BlockSpec((tm, tn), lambda j, i: (i, j)),
        out_shape=jax.ShapeDtypeStruct((bt, d), F32),
        compiler_params=pltpu.CompilerParams(
            dimension_semantics=("parallel", "parallel"),
            vmem_limit_bytes=VMEM_LIMIT),
        name="oproj",
    )(merged, wo, h)


def _mlp_kernel(h_ref, nw_ref, wu_ref, wd_ref, fw_ref, out_ref, xn_scr, acc_scr, *, final):
    k = pl.program_id(1)

    @pl.when(k == 0)
    def _():
        xn_scr[...] = (_rms(h_ref[...]) * nw_ref[...]).astype(BF16)
        acc_scr[...] = jnp.zeros_like(acc_scr)

    up = _dot(xn_scr[...], wu_ref[...])
    act = jnp.square(jnp.maximum(up, 0.0)).astype(BF16)
    acc_scr[...] += _dot(act, wd_ref[...])

    @pl.when(k == pl.num_programs(1) - 1)
    def _():
        y = h_ref[...] + acc_scr[...]
        if final:
            y = _rms(y) * fw_ref[...]
        out_ref[...] = y


def _mlp(h, norm_w, w_up, w_down, final_w, final, tm=512, th=1024):
    bt, d = h.shape
    hid = w_up.shape[1]
    return pl.pallas_call(
        functools.partial(_mlp_kernel, final=final),
        grid=(bt // tm, hid // th),
        in_specs=[
            pl.BlockSpec((tm, d), lambda i, k: (i, 0)),
            pl.BlockSpec((1, d), lambda i, k: (0, 0)),
            pl.BlockSpec((d, th), lambda i, k: (0, k)),
            pl.BlockSpec((th, d), lambda i, k: (k, 0)),
            pl.BlockSpec((1, d), lambda i, k: (0, 0)),
        ],
        out_specs=pl.BlockSpec((tm, d), lambda i, k: (i, 0)),
        out_shape=jax.ShapeDtypeStruct((bt, d), F32),
        scratch_shapes=[pltpu.VMEM((tm, d), BF16), pltpu.VMEM((tm, d), F32)],
        compiler_params=pltpu.CompilerParams(
            dimension_semantics=("parallel", "arbitrary"),
            vmem_limit_bytes=VMEM_LIMIT),
        name="mlp",
    )(h, norm_w, w_up, w_down, final_w)


def kernel(x, norm1_w, w_in, gdn_conv_w, gdn_a_log, gdn_dt_bias, gdn_norm_w, ssm_conv_w, ssm_conv_b, ssm_a_log, ssm_dt_bias, ssm_d, ssm_norm_w, w_gdn_out, w_ssm_out, w_o, norm2_w, w_up, w_down, final_norm_w):
    batch, seq, d = x.shape
    depth = w_in.shape[0]
    assert depth >= 1
    bt = batch * seq
    gw = GDN_HEADS * GDN_DK
    inner = SSM_HEADS * SSM_P
    conv_ch = inner + 2 * SSM_GROUPS * SSM_N
    sizes = (3 * gw, gw, GDN_HEADS, GDN_HEADS, inner, conv_ch, SSM_HEADS, d, d)
    offs = [0]
    for s in sizes:
        offs.append(offs[-1] + s)
    o_qkv, o_gz, o_gb, o_ga, o_sz, o_xbc, o_dt, o_gate_a, o_gate_b, _ = offs
    no_w = jnp.zeros((CONV_K, max(gw, inner, d)), F32)
    no_b = jnp.zeros((1, max(2 * gw, conv_ch, d)), F32)
    zpad = jnp.zeros((LANES - 2 * GDN_HEADS - SSM_HEADS,), F32)

    h = x.reshape(bt, d)
    for l in range(depth):
        assert o_ga == o_gb + GDN_HEADS
        w_qk, w_v, w_xbc, w_gz, w_sz, w_ga, w_gb, w_small = _wprep(
            jnp.swapaxes(w_in[l], 0, 1),
            segs=((o_qkv, 2 * gw), (o_qkv + 2 * gw, gw), (o_xbc, conv_ch), (o_gz, gw), (o_sz, inner),
                  (o_gate_a, d), (o_gate_b, d)),
            smalls=((o_gb, 2 * GDN_HEADS), (o_dt, SSM_HEADS)))

        xn = _norm(h, norm1_w[l][None, :])
        proj = functools.partial(_proj, xn, seq=seq)
        qk = proj(w_qk, gdn_conv_w[l][:, :2 * gw], no_b, conv=True, act="silu",
                  l2norm=True, q_tiles=gw // PROJ_TN, name="proj_qk")
        v = proj(w_v, gdn_conv_w[l][:, 2 * gw:], no_b, conv=True, act="silu",
                 l2norm=False, q_tiles=0, name="proj_v")
        xbc = proj(w_xbc, ssm_conv_w[l], ssm_conv_b[l][None, :], conv=True, act="silu",
                   l2norm=False, q_tiles=0, name="proj_xbc")
        gz = proj(w_gz, no_w, no_b, conv=False, act="silu", l2norm=False, q_tiles=0, name="proj_gz")
        sz = proj(w_sz, no_w, no_b, conv=False, act="silu", l2norm=False, q_tiles=0, name="proj_sz")
        ga = proj(w_ga, no_w, no_b, conv=False, act="sigmoid", l2norm=False, q_tiles=0,
                  name="proj_ga")
        gb = proj(w_gb, no_w, no_b, conv=False, act="sigmoid", l2norm=False, q_tiles=0,
                  name="proj_gb")

        params = jnp.zeros((8, LANES), F32)
        params = params.at[0].set(jnp.concatenate(
            [jnp.zeros((GDN_HEADS,), F32), gdn_dt_bias[l], ssm_dt_bias[l], zpad]))
        params = params.at[1].set(jnp.concatenate(
            [jnp.zeros((GDN_HEADS,), F32), gdn_a_log[l], ssm_a_log[l], zpad]))
        g_cols, g_rows, m_cols, m_rows = _smallprep(xn, w_small, params)

        o = _gdn(qk, v, gz, g_cols, g_rows, gdn_norm_w[l][None, :], batch, seq)
        dskip = jnp.repeat(ssm_d[l], SSM_P)[None, :]
        y = _ssd(xbc, sz, m_cols, m_rows, dskip, ssm_norm_w[l][None, :], batch, seq, 0)
        merged = _merge(o, y, w_gdn_out[l].astype(BF16), w_ssm_out[l].astype(BF16), ga, gb)
        h = _oproj(merged, w_o[l].astype(BF16), h)
        h = _mlp(h, norm2_w[l][None, :], w_up[l].astype(BF16), w_down[l].astype(BF16),
                 final_norm_w[None, :], final=(l == depth - 1))
    return h.reshape(batch, seq, d)
```

```python
import functools

import jax
import jax.numpy as jnp
from jax import lax
from jax.experimental import pallas as pl
from jax.experimental.pallas import tpu as pltpu

F32 = jnp.float32
BF16 = jnp.bfloat16
HIGHEST = lax.Precision.HIGHEST
EPS = 1e-6

LANES = 128
CONV_K = 4
GDN_HEADS = 16
GDN_DK = 128
GDN_CHUNK = 128
GDN_HB = 4
GDN_CPI = 4
SSM_HEADS = 32
SSM_P = 64
SSM_GROUPS = 8
SSM_N = 128
SSM_CHUNK = 128
SSM_R = SSM_HEADS // SSM_GROUPS
SSD_CPI = 4
GDN_TB = 512
SSD_TB = 1024
PROJ_TN = 2048
PROJ_SUB = 256
PROJ_ROWS = 64
PROJ_MSPLIT = 2
VMEM_LIMIT = 48 * 1024 * 1024

LANE_BETA = 0
LANE_GDEC = 16
LANE_DT = 32


def _sigmoid(x):
    return 1.0 / (1.0 + jnp.exp(-x))


def _silu(x):
    return x * _sigmoid(x)


def _dot(a, b, precision=None):
    return jnp.dot(a, b, preferred_element_type=F32, precision=precision)


def _dot3(a, b):
    ah = a.astype(BF16)
    al = (a - ah.astype(F32)).astype(BF16)
    bh = b.astype(BF16)
    bl = (b - bh.astype(F32)).astype(BF16)
    return _dot(ah, bh) + (_dot(ah, bl) + _dot(al, bh))


def _dot_nt(a, b):
    return lax.dot_general(a, b, (((1,), (1,)), ((), ())), preferred_element_type=F32)


def _dot_tn(a, b):
    return lax.dot_general(a, b, (((0,), (0,)), ((), ())), preferred_element_type=F32)


def _rms(x):
    return x * lax.rsqrt(jnp.mean(x * x, axis=-1, keepdims=True) + EPS)


def _wprep_kernel(w_ref, *o_refs, segs, smalls):
    for (a, n), o_ref in zip(segs, o_refs[:-1]):
        o_ref[...] = w_ref[a:a + n, :].astype(o_ref.dtype)
    parts = [w_ref[a:a + n, :] for a, n in smalls]
    parts.append(jnp.zeros((LANES - sum(n for _, n in smalls), w_ref.shape[1]), F32))
    o_refs[-1][...] = jnp.concatenate(parts, axis=0).astype(o_refs[-1].dtype)


def _wprep(wt, segs, smalls, tc=LANES):
    n_all, d = wt.shape
    heights = [n for _, n in segs] + [LANES]
    return pl.pallas_call(
        functools.partial(_wprep_kernel, segs=segs, smalls=smalls),
        grid=(d // tc,),
        in_specs=[pl.BlockSpec((n_all, tc), lambda i: (0, i))],
        out_specs=[pl.BlockSpec((n, tc), lambda i: (0, i)) for n in heights],
        out_shape=[jax.ShapeDtypeStruct((n, d), BF16) for n in heights],
        compiler_params=pltpu.CompilerParams(
            dimension_semantics=("parallel",), vmem_limit_bytes=VMEM_LIMIT),
        name="wprep",
    )(wt)


def _proj_kernel(xn_ref, w_ref, cw_ref, cb_ref, o_ref, *acc_scrs, conv, act, l2norm, q_tiles, seq):
    j = pl.program_id(0)
    i = pl.program_id(1)
    tm, tn = o_ref.shape
    nsub = tn // PROJ_SUB
    if conv:
        @pl.when((i * tm) % seq == 0)
        def _():
            for s in range(nsub):
                acc_scrs[s][0:8, :] = jnp.zeros((8, PROJ_SUB), F32)
    if l2norm:
        scale = jnp.where(j < q_tiles, GDN_DK ** -0.5, 1.0)
    z0 = pl.multiple_of(jnp.minimum(i, 0), 8)

    mh = tm // PROJ_MSPLIT

    def matmul(s, m):
        acc_scrs[s][8 + m * mh:8 + (m + 1) * mh, :] = _dot_nt(
            xn_ref[m * mh:(m + 1) * mh, :], w_ref[s * PROJ_SUB:(s + 1) * PROJ_SUB, :])

    def epilogue(s, m):
        acc_scr = acc_scrs[s]
        pieces = [(rc * PROJ_ROWS, lc * LANES)
                  for rc in range(m * mh // PROJ_ROWS, (m + 1) * mh // PROJ_ROWS)
                  for lc in range(PROJ_SUB // LANES)]
        for r0, l0 in pieces:
            sub = slice(l0, l0 + LANES)
            cols = slice(s * PROJ_SUB + l0, s * PROJ_SUB + l0 + LANES)
            if conv:
                win = acc_scr[pl.ds(z0 + r0, PROJ_ROWS + 8), sub]
                prev = pltpu.roll(win, 1, axis=0)
                lo = win * cw_ref[1:2, cols] + prev * cw_ref[0:1, cols]
                y = (win * cw_ref[3:4, cols] + prev * cw_ref[2:3, cols]
                     + pltpu.roll(lo, 2, axis=0))[8:] + cb_ref[:, cols]
            else:
                y = acc_scr[pl.ds(z0 + 8 + r0, PROJ_ROWS), sub]
            y = _silu(y) if act == "silu" else _sigmoid(y)
            if l2norm:
                y = y * (lax.rsqrt(jnp.sum(y * y, axis=-1, keepdims=True) + EPS) * scale)
            o_ref[r0:r0 + PROJ_ROWS, cols] = y.astype(o_ref.dtype)
        if conv and m == PROJ_MSPLIT - 1:
            acc_scr[0:8, :] = acc_scr[tm:tm + 8, :]

    units = [(s, m) for s in range(nsub) for m in range(PROJ_MSPLIT)]
    ahead = 2
    for u in units[:ahead]:
        matmul(*u)
    for n_, u in enumerate(units):
        if n_ + ahead < len(units):
            matmul(*units[n_ + ahead])
        epilogue(*u)


def _proj(xn, wt, cw, cb, *, conv, act, l2norm, q_tiles, seq, name, tm=1024, tn=PROJ_TN):
    bt, d = xn.shape
    n = wt.shape[0]
    assert seq % tm == 0 and n % tn == 0
    body = functools.partial(_proj_kernel, conv=conv, act=act, l2norm=l2norm, q_tiles=q_tiles, seq=seq)
    return pl.pallas_call(
        body,
        grid=(n // tn, bt // tm),
        in_specs=[
            pl.BlockSpec((tm, d), lambda j, i: (i, 0)),
            pl.BlockSpec((tn, d), lambda j, i: (j, 0)),
            pl.BlockSpec((CONV_K, tn), lambda j, i: (0, j)),
            pl.BlockSpec((1, tn), lambda j, i: (0, j)),
        ],
        out_specs=pl.BlockSpec((tm, tn), lambda j, i: (i, j)),
        out_shape=jax.ShapeDtypeStruct((bt, n), BF16),
        scratch_shapes=[pltpu.VMEM((tm + 8, PROJ_SUB), F32) for _ in range(tn // PROJ_SUB)],
        compiler_params=pltpu.CompilerParams(
            dimension_semantics=("arbitrary", "arbitrary"),
            vmem_limit_bytes=VMEM_LIMIT),
        name=name,
    )(xn, wt, cw, cb)


def _smallprep_kernel(x_ref, nw_ref, w_ref, p_ref, xn_ref, gc_ref, gr_ref, mc_ref, mr_ref):
    tm = x_ref.shape[0]
    xn_ref[...] = (_rms(x_ref[...]) * nw_ref[...]).astype(xn_ref.dtype)
    bias = p_ref[0:1, :]
    neg_a = -jnp.exp(p_ref[1:2, :])
    ii = lax.broadcasted_iota(jnp.int32, (LANES, LANES), 0)
    jj = lax.broadcasted_iota(jnp.int32, (LANES, LANES), 1)
    tril = jnp.where(ii >= jj, 1.0, 0.0).astype(F32)
    ones = jnp.ones((LANES, LANES), F32)
    logits = _dot_nt(xn_ref[...], w_ref[...])
    for s in range(tm // LANES):
        rows = slice(s * LANES, (s + 1) * LANES)
        x = logits[rows, :]
        xb = x + bias
        sp = jnp.maximum(xb, 0.0) + jnp.log1p(jnp.exp(-jnp.abs(xb)))
        dec = neg_a * sp
        cum = _dot(tril, dec, HIGHEST)
        last = _dot(ones, dec, HIGHEST)
        beta = _sigmoid(x)
        ecum = jnp.exp(cum)
        etail = jnp.exp(last - cum)
        elast = jnp.exp(last)
        for f, val in enumerate((beta, cum, ecum, etail)):
            gc_ref[f, rows, :] = val
        for f, val in enumerate((beta, cum, ecum, elast)):
            gr_ref[f, s] = val.T
        for f, val in enumerate((cum, ecum)):
            mc_ref[f, rows, :] = val
        for f, val in enumerate((sp, cum, sp * etail, elast)):
            mr_ref[f, s] = val.T


def _smallprep(x2, norm_w, w_small, params, tm=512):
    bt, d = x2.shape
    nb = tm // LANES

    def cspec(nf):
        return pl.BlockSpec((nf, tm, LANES), lambda i: (0, i, 0))

    def rspec(nf):
        return pl.BlockSpec((nf, nb, LANES, LANES), lambda i: (0, i, 0, 0))

    def cshape(nf):
        return jax.ShapeDtypeStruct((nf, bt, LANES), F32)

    def rshape(nf):
        return jax.ShapeDtypeStruct((nf, bt // LANES, LANES, LANES), F32)

    return pl.pallas_call(
        _smallprep_kernel,
        grid=(bt // tm,),
        in_specs=[pl.BlockSpec((tm, d), lambda i: (i, 0)),
                  pl.BlockSpec((1, d), lambda i: (0, 0)),
                  pl.BlockSpec((LANES, d), lambda i: (0, 0)),
                  pl.BlockSpec((8, LANES), lambda i: (0, 0))],
        out_specs=[pl.BlockSpec((tm, d), lambda i: (i, 0)), cspec(4), rspec(4), cspec(2), rspec(4)],
        out_shape=[jax.ShapeDtypeStruct((bt, d), BF16), cshape(4), rshape(4), cshape(2), rshape(4)],
        compiler_params=pltpu.CompilerParams(dimension_semantics=("parallel",)),
        name="smallprep",
    )(x2, norm_w, w_small, params)


def _col(field, lane_iota, lane):
    return jnp.sum(jnp.where(lane_iota == lane, field, 0.0), axis=-1, keepdims=True)


def _gdn_kernel(q_ref, k_ref, v_ref, z_ref, sm_ref, smt_ref, nw_ref,
                o_ref,
                s_scr, qp_scr, op_scr, phi_scr, psi_scr, egl_scr):
    tb = pl.program_id(1)
    hg = pl.program_id(2)
    tbs = q_ref.shape[0]
    nch = tbs // GDN_CHUNK
    c = GDN_CHUNK
    dk = GDN_DK

    @pl.when(tb == 0)
    def _():
        s_scr[pl.ds(hg * GDN_HB, GDN_HB)] = jnp.zeros((GDN_HB, dk, dk), F32)

    ii = lax.broadcasted_iota(jnp.int32, (c, c), 0)
    jj = lax.broadcasted_iota(jnp.int32, (c, c), 1)
    causal = ii >= jj
    strict = ii > jj
    nlev = 4
    blk = {w: (ii // w) == (jj // w) for w in (16, 32, 64, 128) if w < c}
    ring = {w: jnp.logical_and((ii // (2 * w)) == (jj // (2 * w)), (ii // w) != (jj // w))
            for w in blk}
    lane_iota = lax.broadcasted_iota(jnp.int32, (c, LANES), 1)
    q_lane = lax.broadcasted_iota(jnp.int32, (16, c), 1)
    q_blk = q_lane // 16
    q_eye = jnp.where(q_lane % 16 == lax.broadcasted_iota(jnp.int32, (16, c), 0), 1.0, 0.0).astype(F32)
    cpi = GDN_CPI

    def phase_a(it, carry):
        st = []
        for cc in range(cpi):
            ci = it * cpi + cc
            rows = pl.ds(pl.multiple_of(ci * c, c), c)
            sm = sm_ref[:, rows, :]
            for r in range(GDN_HB):
                h = hg * GDN_HB + r
                cols = slice(r * dk, (r + 1) * dk)
                gc_r, egl_r = (smt_ref[f, ci, pl.ds(LANE_GDEC + h, 1), :] for f in (1, 3))
                st.append(dict(
                    rows=rows, ci=ci, r=r, h=h, sm=sm, qb=q_ref[rows, cols], kb=k_ref[rows, cols],
                    vb=v_ref[rows, cols], gc_r=gc_r, egl=jnp.broadcast_to(egl_r, (8, dk))))
        grams = [_dot_nt(jnp.concatenate([s_["kb"], s_["qb"]], axis=0), s_["kb"]) for s_ in st]
        for f, name, lane0 in ((0, "beta_c", LANE_BETA), (1, "gc_c", LANE_GDEC),
                               (2, "egc_c", LANE_GDEC), (3, "ekl_c", LANE_GDEC)):
            for s_ in st:
                s_[name] = _col(s_["sm"][f], lane_iota, lane0 + s_["h"])
        decs = [jnp.where(causal, jnp.exp(jnp.where(causal, s_["gc_c"] - s_["gc_r"], 0.0)), 0.0)
                for s_ in st]
        for s_, gram, dec in zip(st, grams, decs):
            s_["a"] = jnp.where(strict, gram[:c] * s_["beta_c"] * dec, 0.0)
            s_["attn"] = (gram[c:] * dec).astype(BF16)
        for s_ in st:
            s_["qg"] = s_["qb"].astype(F32) * s_["egc_c"]
            s_["kd"] = (s_["kb"].astype(F32) * s_["ekl_c"]).astype(BF16)

        widths = sorted(blk)

        def expand(q):
            return jnp.where(blk[16], jnp.concatenate([q] * (c // 16), axis=0), 0.0)

        qm, qp = [], []
        for s_ in st:
            acc = None
            for b in range(c // 16):
                term = jnp.where(q_blk == b, s_["a"][16 * b:16 * (b + 1), :], 0.0)
                acc = term if acc is None else acc + term
            qm.append(-acc)
            qp.append(q_eye)
        for lev in range(nlev):
            wm = [expand(m_) for m_ in qm]
            if lev + 1 < nlev:
                outs = [_dot3(jnp.concatenate([m_, p_], axis=0), w_) for m_, p_, w_ in zip(qm, qp, wm)]
                qm = [o_[:16] for o_ in outs]
                qp = [p_ + o_[16:] for p_, o_ in zip(qp, outs)]
            else:
                qp = [p_ + _dot3(p_, w_) for p_, w_ in zip(qp, wm)]
        tinv = [expand(p_) for p_ in qp]
        for w_blk in widths:
            nb = c // (2 * w_blk)
            eb = [jnp.where(ring[w_blk], s_["a"], 0.0).astype(BF16) for s_ in st]
            low = [slice((2 * b + 1) * w_blk, (2 * b + 2) * w_blk) for b in range(nb)]
            upp = [slice(2 * b * w_blk, (2 * b + 1) * w_blk) for b in range(nb)]
            tl = [jnp.concatenate([t_[sl] for sl in low], axis=0) for t_ in tinv]
            tb_ = [t_.astype(BF16) for t_ in tinv]
            te = [_dot(l_.astype(BF16), e_).astype(BF16) for l_, e_ in zip(tl, eb)]
            tl = [l_ - _dot(e_, b_) for l_, e_, b_ in zip(tl, te, tb_)]
            tinv = [jnp.concatenate(
                [piece for b in range(nb) for piece in (t_[upp[b]], l_[b * w_blk:(b + 1) * w_blk])],
                axis=0) for t_, l_ in zip(tinv, tl)]

        rhs = [jnp.concatenate(
            [s_["kb"].astype(F32) * (s_["beta_c"] * s_["egc_c"]), s_["vb"].astype(F32) * s_["beta_c"]],
            axis=1).astype(BF16) for s_ in st]
        wus = [_dot(t_.astype(BF16), r_).astype(BF16) for t_, r_ in zip(tinv, rhs)]
        aws = [_dot(s_["attn"], wu_) for s_, wu_ in zip(st, wus)]
        kws = [_dot_tn(s_["kd"], wu_) for s_, wu_ in zip(st, wus)]
        for s_, aw, kw in zip(st, aws, kws):
            r, rows, ci = s_["r"], s_["rows"], s_["ci"]
            qp_scr[r, rows, :] = (s_["qg"] - aw[:, :dk]).astype(BF16)
            op_scr[r, rows, :] = aw[:, dk:]
            phi_scr[r, ci] = (-kw[:, :dk]).astype(BF16)
            psi_scr[r, ci] = kw[:, dk:]
            egl_scr[r, ci] = s_["egl"]
        return carry

    lax.fori_loop(0, nch // cpi, phase_a, 0)

    nw = nw_ref[...]

    heads = range(GDN_HB)
    ss = [s_scr[hg * GDN_HB + r] for r in heads]
    for ci in range(nch):
        rows = slice(ci * c, (ci + 1) * c)
        sb = [s_.astype(BF16) for s_ in ss]
        ds = [_dot(phi_scr[r, ci], sb[r]) for r in heads]
        ss = [ss[r] * egl_scr[r, ci, 0:1, :] + ds[r] + psi_scr[r, ci] for r in heads]
        os_ = [_dot(qp_scr[r, rows, :], sb[r]) + op_scr[r, rows, :] for r in heads]
        ons = [_rms(o) * nw for o in os_]
        for r in heads:
            cols = slice(r * dk, (r + 1) * dk)
            o_ref[rows, cols] = (ons[r] * z_ref[rows, cols].astype(F32)).astype(o_ref.dtype)
    for r in heads:
        s_scr[hg * GDN_HB + r] = ss[r]


def _gdn(qk, v, z, g_fields, g_rows, norm_w, batch, seq):
    bt = qk.shape[0]
    tbs = GDN_TB
    ntb = seq // tbs
    nhg = GDN_HEADS // GDN_HB
    cw = GDN_HB * GDN_DK
    width = GDN_HEADS * GDN_DK
    row = lambda b, t, h: b * ntb + t
    return pl.pallas_call(
        _gdn_kernel,
        grid=(batch, ntb, nhg),
        in_specs=[
            pl.BlockSpec((tbs, cw), lambda b, t, h: (row(b, t, h), h)),
            pl.BlockSpec((tbs, cw), lambda b, t, h: (row(b, t, h), nhg + h)),
            pl.BlockSpec((tbs, cw), lambda b, t, h: (row(b, t, h), h)),
            pl.BlockSpec((tbs, cw), lambda b, t, h: (row(b, t, h), h)),
            pl.BlockSpec((4, tbs, LANES), lambda b, t, h: (0, row(b, t, h), 0)),
            pl.BlockSpec((4, tbs // LANES, LANES, LANES), lambda b, t, h: (0, row(b, t, h), 0, 0)),
            pl.BlockSpec((1, GDN_DK), lambda b, t, h: (0, 0)),
        ],
        out_specs=pl.BlockSpec((tbs, cw), lambda b, t, h: (row(b, t, h), h)),
        out_shape=jax.ShapeDtypeStruct((bt, width), BF16),
        scratch_shapes=[
            pltpu.VMEM((GDN_HEADS, GDN_DK, GDN_DK), F32),
            pltpu.VMEM((GDN_HB, tbs, GDN_DK), BF16),
            pltpu.VMEM((GDN_HB, tbs, GDN_DK), F32),
            pltpu.VMEM((GDN_HB, tbs // GDN_CHUNK, GDN_DK, GDN_DK), BF16),
            pltpu.VMEM((GDN_HB, tbs // GDN_CHUNK, GDN_DK, GDN_DK), F32),
            pltpu.VMEM((GDN_HB, tbs // GDN_CHUNK, 8, GDN_DK), F32),
        ],
        compiler_params=pltpu.CompilerParams(
            dimension_semantics=("arbitrary", "arbitrary", "arbitrary"),
            vmem_limit_bytes=VMEM_LIMIT),
        name="gdn",
    )(qk, qk, v, z, g_fields, g_rows, norm_w)


def _ssd_kernel(x_ref, b_ref, c_ref, z_ref, sm_ref, smt_ref, dsk_ref, nw_ref,
                o_ref,
                h_scr):
    tb = pl.program_id(1)
    g = pl.program_id(2)
    tbs = x_ref.shape[0]
    l = SSM_CHUNK
    p = SSM_P
    nch = tbs // l

    @pl.when(tb == 0)
    def _():
        h_scr[g] = jnp.zeros(h_scr.shape[1:], F32)

    ii = lax.broadcasted_iota(jnp.int32, (l, l), 0)
    jj = lax.broadcasted_iota(jnp.int32, (l, l), 1)
    causal = ii >= jj
    lane_iota = lax.broadcasted_iota(jnp.int32, (l, LANES), 1)
    dsk = dsk_ref[...]
    nw = nw_ref[...]

    first_of_pair = lane_iota < p
    npair = SSM_R // 2
    cpi = SSD_CPI

    def pick(a0, a1):
        return jnp.where(first_of_pair[:a0.shape[0]], a0, a1)

    def body(it, carry):
        h = h_scr[g]
        st = []
        for cc_ in range(cpi):
            ci = it * cpi + cc_
            rows = pl.ds(pl.multiple_of(ci * l, l), l)
            bc = b_ref[rows, :]
            cm = c_ref[rows, :]
            sm = sm_ref[:, rows, :]
            rowf = [[smt_ref[f, ci, pl.ds(LANE_DT + g * SSM_R + r, 1), :] for f in range(4)]
                    for r in range(SSM_R)]
            st.append(dict(rows=rows, xb=x_ref[rows, :], cm=cm, sm=sm, rowf=rowf,
                           scores=_dot_nt(cm, bc), bct=bc.astype(F32).T))
        for s_ in st:
            s_["ac_c"] = [_col(s_["sm"][0], lane_iota, LANE_DT + g * SSM_R + r) for r in range(SSM_R)]
            s_["eac_c"] = [_col(s_["sm"][1], lane_iota, LANE_DT + g * SSM_R + r) for r in range(SSM_R)]
        for s_ in st:
            s_["m"] = [(s_["scores"] * jnp.where(
                causal, jnp.exp(jnp.where(causal, s_["ac_c"][r] - s_["rowf"][r][1], 0.0)), 0.0)
                * s_["rowf"][r][0]).astype(BF16) for r in range(SSM_R)]
            s_["bt"] = [(s_["bct"] * s_["rowf"][r][2]).astype(BF16) for r in range(SSM_R)]
        for s_ in st:
            yd, stt, eac, eal = [], [], [], []
            for pr in range(npair):
                xp = s_["xb"][:, pr * LANES:(pr + 1) * LANES]
                r0_, r1_ = 2 * pr, 2 * pr + 1
                yd.append(pick(_dot(s_["m"][r0_], xp), _dot(s_["m"][r1_], xp)))
                stt.append(pick(_dot(s_["bt"][r0_], xp), _dot(s_["bt"][r1_], xp)))
                eac.append(pick(jnp.broadcast_to(s_["eac_c"][r0_], (l, LANES)),
                                jnp.broadcast_to(s_["eac_c"][r1_], (l, LANES))))
                eal.append(pick(s_["rowf"][r0_][3], s_["rowf"][r1_][3]))
            s_["yd"] = jnp.concatenate(yd, axis=1)
            s_["states"] = jnp.concatenate(stt, axis=1)
            s_["eac"] = jnp.concatenate(eac, axis=1)
            s_["eal"] = jnp.concatenate(eal, axis=1)
        for s_ in st:
            y_off = _dot(s_["cm"], h.astype(BF16))
            h = h * s_["eal"] + s_["states"]
            y = (s_["yd"] + y_off * s_["eac"] + dsk * s_["xb"].astype(F32)) \
                * z_ref[s_["rows"], :].astype(F32)
            o_ref[s_["rows"], :] = (_rms(y) * nw).astype(o_ref.dtype)
        h_scr[g] = h
        return carry

    lax.fori_loop(0, nch // cpi, body, 0)


def _ssd(xbc, z, m_fields, m_rows, dskip, norm_w, batch, seq, z_col0):
    bt = xbc.shape[0]
    tbs = SSD_TB
    ntb = seq // tbs
    ng = SSM_GROUPS
    gw = SSM_R * SSM_P
    inner = SSM_HEADS * SSM_P
    bb0 = inner // SSM_N
    cb0 = (inner + ng * SSM_N) // SSM_N
    zb0 = z_col0 // gw
    row = lambda b, t, g: b * ntb + t
    return pl.pallas_call(
        _ssd_kernel,
        grid=(batch, ntb, ng),
        in_specs=[
            pl.BlockSpec((tbs, gw), lambda b, t, g: (row(b, t, g), g)),
            pl.BlockSpec((tbs, SSM_N), lambda b, t, g: (row(b, t, g), bb0 + g)),
            pl.BlockSpec((tbs, SSM_N), lambda b, t, g: (row(b, t, g), cb0 + g)),
            pl.BlockSpec((tbs, gw), lambda b, t, g: (row(b, t, g), zb0 + g)),
            pl.BlockSpec((2, tbs, LANES), lambda b, t, g: (0, row(b, t, g), 0)),
            pl.BlockSpec((4, tbs // SSM_CHUNK, LANES, SSM_CHUNK),
                         lambda b, t, g: (0, row(b, t, g), 0, 0)),
            pl.BlockSpec((1, gw), lambda b, t, g: (0, g)),
            pl.BlockSpec((1, gw), lambda b, t, g: (0, g)),
        ],
        out_specs=pl.BlockSpec((tbs, gw), lambda b, t, g: (row(b, t, g), g)),
        out_shape=jax.ShapeDtypeStruct((bt, inner), BF16),
        scratch_shapes=[pltpu.VMEM((ng, SSM_N, gw), F32)],
        compiler_params=pltpu.CompilerParams(
            dimension_semantics=("arbitrary", "arbitrary", "arbitrary"),
            vmem_limit_bytes=VMEM_LIMIT),
        name="ssd",
    )(xbc, xbc, xbc, z, m_fields, m_rows, dskip, norm_w)


def _merge_kernel(o_ref, y_ref, wg_ref, ws_ref, ga_ref, gb_ref, out_ref):
    a = _dot(o_ref[...], wg_ref[...])
    b = _dot(y_ref[...], ws_ref[...])
    out_ref[...] = (ga_ref[...].astype(F32) * a + gb_ref[...].astype(F32) * b).astype(out_ref.dtype)


def _merge(o, y, wg, ws, ga, gb, tm=512, tn=1024):
    bt, d_in = o.shape
    d = wg.shape[1]
    return pl.pallas_call(
        _merge_kernel,
        grid=(d // tn, bt // tm),
        in_specs=[
            pl.BlockSpec((tm, d_in), lambda j, i: (i, 0)),
            pl.BlockSpec((tm, d_in), lambda j, i: (i, 0)),
            pl.BlockSpec((d_in, tn), lambda j, i: (0, j)),
            pl.BlockSpec((d_in, tn), lambda j, i: (0, j)),
            pl.BlockSpec((tm, tn), lambda j, i: (i, j)),
            pl.BlockSpec((tm, tn), lambda j, i: (i, j)),
        ],
        out_specs=pl.BlockSpec((tm, tn), lambda j, i: (i, j)),
        out_shape=jax.ShapeDtypeStruct((bt, d), BF16),
        compiler_params=pltpu.CompilerParams(
            dimension_semantics=("parallel", "parallel"),
            vmem_limit_bytes=VMEM_LIMIT),
        name="merge",
    )(o, y, wg, ws, ga, gb)


def _oproj_kernel(m_ref, w_ref, h_ref, out_ref):
    out_ref[...] = h_ref[...] + _dot(m_ref[...], w_ref[...])


def _oproj(merged, wo, h, tm=1024, tn=1024):
    bt, d_in = merged.shape
    d = wo.shape[1]
    return pl.pallas_call(
        _oproj_kernel,
        grid=(d // tn, bt // tm),
        in_specs=[
            pl.BlockSpec((tm, d_in), lambda j, i: (i, 0)),
            pl.BlockSpec((d_in, tn), lambda j, i: (0, j)),
            pl.BlockSpec((tm, tn), lambda j, i: (i, j)),
        ],
        out_specs=pl.BlockSpec((tm, tn), lambda j, i: (i, j)),
        out_shape=jax.ShapeDtypeStruct((bt, d), F32),
        compiler_params=pltpu.CompilerParams(
            dimension_semantics=("parallel", "parallel"),
            vmem_limit_bytes=VMEM_LIMIT),
        name="oproj",
    )(merged, wo, h)


def _mlp_kernel(h_ref, nw_ref, wu_ref, wd_ref, fw_ref, out_ref, xn_scr, acc_scr, *, final):
    k = pl.program_id(1)

    @pl.when(k == 0)
    def _():
        xn_scr[...] = (_rms(h_ref[...]) * nw_ref[...]).astype(BF16)
        acc_scr[...] = jnp.zeros_like(acc_scr)

    up = _dot(xn_scr[...], wu_ref[...])
    act = jnp.square(jnp.maximum(up, 0.0)).astype(BF16)
    acc_scr[...] += _dot(act, wd_ref[...])

    @pl.when(k == pl.num_programs(1) - 1)
    def _():
        y = h_ref[...] + acc_scr[...]
        if final:
            y = _rms(y) * fw_ref[...]
        out_ref[...] = y


def _mlp(h, norm_w, w_up, w_down, final_w, final, tm=512, th=1024):
    bt, d = h.shape
    hid = w_up.shape[1]
    return pl.pallas_call(
        functools.partial(_mlp_kernel, final=final),
        grid=(bt // tm, hid // th),
        in_specs=[
            pl.BlockSpec((tm, d), lambda i, k: (i, 0)),
            pl.BlockSpec((1, d), lambda i, k: (0, 0)),
            pl.BlockSpec((d, th), lambda i, k: (0, k)),
            pl.BlockSpec((th, d), lambda i, k: (k, 0)),
            pl.BlockSpec((1, d), lambda i, k: (0, 0)),
        ],
        out_specs=pl.BlockSpec((tm, d), lambda i, k: (i, 0)),
        out_shape=jax.ShapeDtypeStruct((bt, d), F32),
        scratch_shapes=[pltpu.VMEM((tm, d), BF16), pltpu.VMEM((tm, d), F32)],
        compiler_params=pltpu.CompilerParams(
            dimension_semantics=("parallel", "arbitrary"),
            vmem_limit_bytes=VMEM_LIMIT),
        name="mlp",
    )(h, norm_w, w_up, w_down, final_w)


def kernel(x, norm1_w, w_in, gdn_conv_w, gdn_a_log, gdn_dt_bias, gdn_norm_w, ssm_conv_w, ssm_conv_b, ssm_a_log, ssm_dt_bias, ssm_d, ssm_norm_w, w_gdn_out, w_ssm_out, w_o, norm2_w, w_up, w_down, final_norm_w):
    batch, seq, d = x.shape
    depth = w_in.shape[0]
    assert depth >= 1
    bt = batch * seq
    gw = GDN_HEADS * GDN_DK
    inner = SSM_HEADS * SSM_P
    conv_ch = inner + 2 * SSM_GROUPS * SSM_N
    sizes = (3 * gw, gw, GDN_HEADS, GDN_HEADS, inner, conv_ch, SSM_HEADS, d, d)
    offs = [0]
    for s in sizes:
        offs.append(offs[-1] + s)
    o_qkv, o_gz, o_gb, o_ga, o_sz, o_xbc, o_dt, o_gate_a, o_gate_b, _ = offs
    no_w = jnp.zeros((CONV_K, max(gw, inner, d)), F32)
    no_b = jnp.zeros((1, max(2 * gw, conv_ch, d)), F32)
    zpad = jnp.zeros((LANES - 2 * GDN_HEADS - SSM_HEADS,), F32)

    h = x.reshape(bt, d)
    for l in range(depth):
        assert o_ga == o_gb + GDN_HEADS
        w_qk, w_v, w_xbc, w_gz, w_sz, w_ga, w_gb, w_small = _wprep(
            jnp.swapaxes(w_in[l], 0, 1),
            segs=((o_qkv, 2 * gw), (o_qkv + 2 * gw, gw), (o_xbc, conv_ch), (o_gz, gw), (o_sz, inner),
                  (o_gate_a, d), (o_gate_b, d)),
            smalls=((o_gb, 2 * GDN_HEADS), (o_dt, SSM_HEADS)))

        params = jnp.zeros((8, LANES), F32)
        params = params.at[0].set(jnp.concatenate(
            [jnp.zeros((GDN_HEADS,), F32), gdn_dt_bias[l], ssm_dt_bias[l], zpad]))
        params = params.at[1].set(jnp.concatenate(
            [jnp.zeros((GDN_HEADS,), F32), gdn_a_log[l], ssm_a_log[l], zpad]))
        xn, g_cols, g_rows, m_cols, m_rows = _smallprep(h, norm1_w[l][None, :], w_small, params)

        proj = functools.partial(_proj, xn, seq=seq)
        qk = proj(w_qk, gdn_conv_w[l][:, :2 * gw], no_b, conv=True, act="silu",
                  l2norm=True, q_tiles=gw // PROJ_TN, name="proj_qk")
        v = proj(w_v, gdn_conv_w[l][:, 2 * gw:], no_b, conv=True, act="silu",
                 l2norm=False, q_tiles=0, name="proj_v")
        xbc = proj(w_xbc, ssm_conv_w[l], ssm_conv_b[l][None, :], conv=True, act="silu",
                   l2norm=False, q_tiles=0, name="proj_xbc")
        gz = proj(w_gz, no_w, no_b, conv=False, act="silu", l2norm=False, q_tiles=0, name="proj_gz")
        sz = proj(w_sz, no_w, no_b, conv=False, act="silu", l2norm=False, q_tiles=0, name="proj_sz")
        ga = proj(w_ga, no_w, no_b, conv=False, act="sigmoid", l2norm=False, q_tiles=0,
                  name="proj_ga")
        gb = proj(w_gb, no_w, no_b, conv=False, act="sigmoid", l2norm=False, q_tiles=0,
                  name="proj_gb")

        o = _gdn(qk, v, gz, g_cols, g_rows, gdn_norm_w[l][None, :], batch, seq)
        dskip = jnp.repeat(ssm_d[l], SSM_P)[None, :]
        y = _ssd(xbc, sz, m_cols, m_rows, dskip, ssm_norm_w[l][None, :], batch, seq, 0)
        merged = _merge(o, y, w_gdn_out[l].astype(BF16), w_ssm_out[l].astype(BF16), ga, gb)
        h = _oproj(merged, w_o[l].astype(BF16), h)
        h = _mlp(h, norm2_w[l][None, :], w_up[l].astype(BF16), w_down[l].astype(BF16),
                 final_norm_w[None, :], final=(l == depth - 1))
    return h.reshape(batch, seq, d)
```

```python
import functools

import jax
import jax.numpy as jnp
from jax import lax
from jax.experimental import pallas as pl
from jax.experimental.pallas import tpu as pltpu

F32 = jnp.float32
BF16 = jnp.bfloat16
HIGHEST = lax.Precision.HIGHEST
EPS = 1e-6

LANES = 128
CONV_K = 4
GDN_HEADS = 16
GDN_DK = 128
GDN_CHUNK = 128
GDN_HB = 4
GDN_CPI = 4
SSM_HEADS = 32
SSM_P = 64
SSM_GROUPS = 8
SSM_N = 128
SSM_CHUNK = 128
SSM_R = SSM_HEADS // SSM_GROUPS
SSD_CPI = 4
GDN_TB = 512
SSD_TB = 1024
PROJ_TN = 2048
PROJ_SUB = 256
PROJ_ROWS = 64
PROJ_MSPLIT = 2
VMEM_LIMIT = 48 * 1024 * 1024

LANE_BETA = 0
LANE_GDEC = 16
LANE_DT = 32


def _sigmoid(x):
    return 0.5 * jnp.tanh(0.5 * x) + 0.5


def _silu(x):
    hx = 0.5 * x
    return hx * jnp.tanh(hx) + hx


def _dot(a, b, precision=None):
    return jnp.dot(a, b, preferred_element_type=F32, precision=precision)


def _dot3(a, b):
    ah = a.astype(BF16)
    al = (a - ah.astype(F32)).astype(BF16)
    bh = b.astype(BF16)
    bl = (b - bh.astype(F32)).astype(BF16)
    return _dot(ah, bh) + (_dot(ah, bl) + _dot(al, bh))


def _dot_nt(a, b):
    return lax.dot_general(a, b, (((1,), (1,)), ((), ())), preferred_element_type=F32)


def _dot_tn(a, b):
    return lax.dot_general(a, b, (((0,), (0,)), ((), ())), preferred_element_type=F32)


def _rms(x):
    return x * lax.rsqrt(jnp.mean(x * x, axis=-1, keepdims=True) + EPS)


def _wprep_kernel(w_ref, *o_refs, segs, smalls):
    for (a, n), o_ref in zip(segs, o_refs[:-1]):
        o_ref[...] = w_ref[a:a + n, :].astype(o_ref.dtype)
    parts = [w_ref[a:a + n, :] for a, n in smalls]
    parts.append(jnp.zeros((LANES - sum(n for _, n in smalls), w_ref.shape[1]), F32))
    o_refs[-1][...] = jnp.concatenate(parts, axis=0).astype(o_refs[-1].dtype)


def _wprep(wt, segs, smalls, tc=LANES):
    n_all, d = wt.shape
    heights = [n for _, n in segs] + [LANES]
    return pl.pallas_call(
        functools.partial(_wprep_kernel, segs=segs, smalls=smalls),
        grid=(d // tc,),
        in_specs=[pl.BlockSpec((n_all, tc), lambda i: (0, i))],
        out_specs=[pl.BlockSpec((n, tc), lambda i: (0, i)) for n in heights],
        out_shape=[jax.ShapeDtypeStruct((n, d), BF16) for n in heights],
        compiler_params=pltpu.CompilerParams(
            dimension_semantics=("parallel",), vmem_limit_bytes=VMEM_LIMIT),
        name="wprep",
    )(wt)


def _proj_kernel(xn_ref, w_ref, cw_ref, cb_ref, o_ref, *acc_scrs, conv, act, l2norm, q_tiles, seq):
    j = pl.program_id(0)
    i = pl.program_id(1)
    tm, tn = o_ref.shape
    nsub = tn // PROJ_SUB
    if conv:
        @pl.when((i * tm) % seq == 0)
        def _():
            for s in range(nsub):
                acc_scrs[s][0:8, :] = jnp.zeros((8, PROJ_SUB), F32)
    if l2norm:
        scale = jnp.where(j < q_tiles, GDN_DK ** -0.5, 1.0)
    z0 = pl.multiple_of(jnp.minimum(i, 0), 8)

    mh = tm // PROJ_MSPLIT

    def matmul(s, m):
        acc_scrs[s][8 + m * mh:8 + (m + 1) * mh, :] = _dot_nt(
            xn_ref[m * mh:(m + 1) * mh, :], w_ref[s * PROJ_SUB:(s + 1) * PROJ_SUB, :])

    def epilogue(s, m):
        acc_scr = acc_scrs[s]
        pieces = [(rc * PROJ_ROWS, lc * LANES)
                  for rc in range(m * mh // PROJ_ROWS, (m + 1) * mh // PROJ_ROWS)
                  for lc in range(PROJ_SUB // LANES)]
        for r0, l0 in pieces:
            sub = slice(l0, l0 + LANES)
            cols = slice(s * PROJ_SUB + l0, s * PROJ_SUB + l0 + LANES)
            if conv:
                win = acc_scr[pl.ds(z0 + r0, PROJ_ROWS + 8), sub]
                prev = pltpu.roll(win, 1, axis=0)
                lo = win * cw_ref[1:2, cols] + prev * cw_ref[0:1, cols]
                y = (win * cw_ref[3:4, cols] + prev * cw_ref[2:3, cols]
                     + pltpu.roll(lo, 2, axis=0))[8:] + cb_ref[:, cols]
            else:
                y = acc_scr[pl.ds(z0 + 8 + r0, PROJ_ROWS), sub]
            y = _silu(y) if act == "silu" else _sigmoid(y)
            if l2norm:
                y = y * (lax.rsqrt(jnp.sum(y * y, axis=-1, keepdims=True) + EPS) * scale)
            o_ref[r0:r0 + PROJ_ROWS, cols] = y.astype(o_ref.dtype)
        if conv and m == PROJ_MSPLIT - 1:
            acc_scr[0:8, :] = acc_scr[tm:tm + 8, :]

    units = [(s, m) for s in range(nsub) for m in range(PROJ_MSPLIT)]
    ahead = 2
    for u in units[:ahead]:
        matmul(*u)
    for n_, u in enumerate(units):
        if n_ + ahead < len(units):
            matmul(*units[n_ + ahead])
        epilogue(*u)


def _proj(xn, wt, cw, cb, *, conv, act, l2norm, q_tiles, seq, name, tm=1024, tn=PROJ_TN):
    bt, d = xn.shape
    n = wt.shape[0]
    assert seq % tm == 0 and n % tn == 0
    body = functools.partial(_proj_kernel, conv=conv, act=act, l2norm=l2norm, q_tiles=q_tiles, seq=seq)
    return pl.pallas_call(
        body,
        grid=(n // tn, bt // tm),
        in_specs=[
            pl.BlockSpec((tm, d), lambda j, i: (i, 0)),
            pl.BlockSpec((tn, d), lambda j, i: (j, 0)),
            pl.BlockSpec((CONV_K, tn), lambda j, i: (0, j)),
            pl.BlockSpec((1, tn), lambda j, i: (0, j)),
        ],
        out_specs=pl.BlockSpec((tm, tn), lambda j, i: (i, j)),
        out_shape=jax.ShapeDtypeStruct((bt, n), BF16),
        scratch_shapes=[pltpu.VMEM((tm + 8, PROJ_SUB), F32) for _ in range(tn // PROJ_SUB)],
        compiler_params=pltpu.CompilerParams(
            dimension_semantics=("arbitrary", "arbitrary"),
            vmem_limit_bytes=VMEM_LIMIT),
        name=name,
    )(xn, wt, cw, cb)


def _smallprep_kernel(x_ref, nw_ref, w_ref, p_ref, xn_ref, gc_ref, gr_ref, mc_ref, mr_ref):
    tm = x_ref.shape[0]
    xn_ref[...] = (_rms(x_ref[...]) * nw_ref[...]).astype(xn_ref.dtype)
    bias = p_ref[0:1, :]
    neg_a = -jnp.exp(p_ref[1:2, :])
    ii = lax.broadcasted_iota(jnp.int32, (LANES, LANES), 0)
    jj = lax.broadcasted_iota(jnp.int32, (LANES, LANES), 1)
    tril = jnp.where(ii >= jj, 1.0, 0.0).astype(F32)
    ones = jnp.ones((LANES, LANES), F32)
    logits = _dot_nt(xn_ref[...], w_ref[...])
    for s in range(tm // LANES):
        rows = slice(s * LANES, (s + 1) * LANES)
        x = logits[rows, :]
        xb = x + bias
        sp = jnp.maximum(xb, 0.0) + jnp.log1p(jnp.exp(-jnp.abs(xb)))
        dec = neg_a * sp
        cum = _dot(tril, dec, HIGHEST)
        last = _dot(ones, dec, HIGHEST)
        beta = _sigmoid(x)
        ecum = jnp.exp(cum)
        etail = jnp.exp(last - cum)
        elast = jnp.exp(last)
        for f, val in enumerate((beta, cum, ecum, etail)):
            gc_ref[f, rows, :] = val
        for f, val in enumerate((beta, cum, ecum, elast)):
            gr_ref[f, s] = val.T
        for f, val in enumerate((cum, ecum)):
            mc_ref[f, rows, :] = val
        for f, val in enumerate((sp, cum, sp * etail, elast)):
            mr_ref[f, s] = val.T


def _smallprep(x2, norm_w, w_small, params, tm=512):
    bt, d = x2.shape
    nb = tm // LANES

    def cspec(nf):
        return pl.BlockSpec((nf, tm, LANES), lambda i: (0, i, 0))

    def rspec(nf):
        return pl.BlockSpec((nf, nb, LANES, LANES), lambda i: (0, i, 0, 0))

    def cshape(nf):
        return jax.ShapeDtypeStruct((nf, bt, LANES), F32)

    def rshape(nf):
        return jax.ShapeDtypeStruct((nf, bt // LANES, LANES, LANES), F32)

    return pl.pallas_call(
        _smallprep_kernel,
        grid=(bt // tm,),
        in_specs=[pl.BlockSpec((tm, d), lambda i: (i, 0)),
                  pl.BlockSpec((1, d), lambda i: (0, 0)),
                  pl.BlockSpec((LANES, d), lambda i: (0, 0)),
                  pl.BlockSpec((8, LANES), lambda i: (0, 0))],
        out_specs=[pl.BlockSpec((tm, d), lambda i: (i, 0)), cspec(4), rspec(4), cspec(2), rspec(4)],
        out_shape=[jax.ShapeDtypeStruct((bt, d), BF16), cshape(4), rshape(4), cshape(2), rshape(4)],
        compiler_params=pltpu.CompilerParams(dimension_semantics=("parallel",)),
        name="smallprep",
    )(x2, norm_w, w_small, params)


def _col(field, lane_iota, lane):
    return jnp.sum(jnp.where(lane_iota == lane, field, 0.0), axis=-1, keepdims=True)


def _gdn_kernel(q_ref, k_ref, v_ref, z_ref, sm_ref, smt_ref, nw_ref,
                o_ref,
                s_scr, qp_scr, op_scr, phi_scr, psi_scr, egl_scr):
    tb = pl.program_id(1)
    hg = pl.program_id(2)
    tbs = q_ref.shape[0]
    nch = tbs // GDN_CHUNK
    c = GDN_CHUNK
    dk = GDN_DK

    @pl.when(tb == 0)
    def _():
        s_scr[pl.ds(hg * GDN_HB, GDN_HB)] = jnp.zeros((GDN_HB, dk, dk), F32)

    ii = lax.broadcasted_iota(jnp.int32, (c, c), 0)
    jj = lax.broadcasted_iota(jnp.int32, (c, c), 1)
    causal = ii >= jj
    strict = ii > jj
    nlev = 4
    blk = {w: (ii // w) == (jj // w) for w in (16, 32, 64, 128) if w < c}
    ring = {w: jnp.logical_and((ii // (2 * w)) == (jj // (2 * w)), (ii // w) != (jj // w))
            for w in blk}
    lane_iota = lax.broadcasted_iota(jnp.int32, (c, LANES), 1)
    q_lane = lax.broadcasted_iota(jnp.int32, (16, c), 1)
    q_blk = q_lane // 16
    q_eye = jnp.where(q_lane % 16 == lax.broadcasted_iota(jnp.int32, (16, c), 0), 1.0, 0.0).astype(F32)
    cpi = GDN_CPI

    def phase_a(it, carry):
        st = []
        for cc in range(cpi):
            ci = it * cpi + cc
            rows = pl.ds(pl.multiple_of(ci * c, c), c)
            sm = sm_ref[:, rows, :]
            for r in range(GDN_HB):
                h = hg * GDN_HB + r
                cols = slice(r * dk, (r + 1) * dk)
                gc_r, egl_r = (smt_ref[f, ci, pl.ds(LANE_GDEC + h, 1), :] for f in (1, 3))
                st.append(dict(
                    rows=rows, ci=ci, r=r, h=h, sm=sm, qb=q_ref[rows, cols], kb=k_ref[rows, cols],
                    vb=v_ref[rows, cols], gc_r=gc_r, egl=jnp.broadcast_to(egl_r, (8, dk))))
        grams = [_dot_nt(jnp.concatenate([s_["kb"], s_["qb"]], axis=0), s_["kb"]) for s_ in st]
        for f, name, lane0 in ((0, "beta_c", LANE_BETA), (1, "gc_c", LANE_GDEC),
                               (2, "egc_c", LANE_GDEC), (3, "ekl_c", LANE_GDEC)):
            for s_ in st:
                s_[name] = _col(s_["sm"][f], lane_iota, lane0 + s_["h"])
        decs = [jnp.where(causal, jnp.exp(jnp.where(causal, s_["gc_c"] - s_["gc_r"], 0.0)), 0.0)
                for s_ in st]
        for s_, gram, dec in zip(st, grams, decs):
            s_["a"] = jnp.where(strict, gram[:c] * s_["beta_c"] * dec, 0.0)
            s_["attn"] = (gram[c:] * dec).astype(BF16)
        for s_ in st:
            s_["qg"] = s_["qb"].astype(F32) * s_["egc_c"]
            s_["kd"] = (s_["kb"].astype(F32) * s_["ekl_c"]).astype(BF16)

        widths = sorted(blk)

        def expand(q):
            return jnp.where(blk[16], jnp.concatenate([q] * (c // 16), axis=0), 0.0)

        qm, qp = [], []
        for s_ in st:
            acc = None
            for b in range(c // 16):
                term = jnp.where(q_blk == b, s_["a"][16 * b:16 * (b + 1), :], 0.0)
                acc = term if acc is None else acc + term
            qm.append(-acc)
            qp.append(q_eye)
        for lev in range(nlev):
            wm = [expand(m_) for m_ in qm]
            if lev + 1 < nlev:
                outs = [_dot3(jnp.concatenate([m_, p_], axis=0), w_) for m_, p_, w_ in zip(qm, qp, wm)]
                qm = [o_[:16] for o_ in outs]
                qp = [p_ + o_[16:] for p_, o_ in zip(qp, outs)]
            else:
                qp = [p_ + _dot3(p_, w_) for p_, w_ in zip(qp, wm)]
        tinv = [expand(p_) for p_ in qp]
        for w_blk in widths:
            nb = c // (2 * w_blk)
            eb = [jnp.where(ring[w_blk], s_["a"], 0.0).astype(BF16) for s_ in st]
            low = [slice((2 * b + 1) * w_blk, (2 * b + 2) * w_blk) for b in range(nb)]
            upp = [slice(2 * b * w_blk, (2 * b + 1) * w_blk) for b in range(nb)]
            tl = [jnp.concatenate([t_[sl] for sl in low], axis=0) for t_ in tinv]
            tb_ = [t_.astype(BF16) for t_ in tinv]
            te = [_dot(l_.astype(BF16), e_).astype(BF16) for l_, e_ in zip(tl, eb)]
            tl = [l_ - _dot(e_, b_) for l_, e_, b_ in zip(tl, te, tb_)]
            tinv = [jnp.concatenate(
                [piece for b in range(nb) for piece in (t_[upp[b]], l_[b * w_blk:(b + 1) * w_blk])],
                axis=0) for t_, l_ in zip(tinv, tl)]

        rhs = [jnp.concatenate(
            [s_["kb"].astype(F32) * (s_["beta_c"] * s_["egc_c"]), s_["vb"].astype(F32) * s_["beta_c"]],
            axis=1).astype(BF16) for s_ in st]
        wus = [_dot(t_.astype(BF16), r_).astype(BF16) for t_, r_ in zip(tinv, rhs)]
        aws = [_dot(s_["attn"], wu_) for s_, wu_ in zip(st, wus)]
        kws = [_dot_tn(s_["kd"], wu_) for s_, wu_ in zip(st, wus)]
        for s_, aw, kw in zip(st, aws, kws):
            r, rows, ci = s_["r"], s_["rows"], s_["ci"]
            qp_scr[r, rows, :] = (s_["qg"] - aw[:, :dk]).astype(BF16)
            op_scr[r, rows, :] = aw[:, dk:]
            phi_scr[r, ci] = (-kw[:, :dk]).astype(BF16)
            psi_scr[r, ci] = kw[:, dk:]
            egl_scr[r, ci] = s_["egl"]
        return carry

    lax.fori_loop(0, nch // cpi, phase_a, 0)

    nw = nw_ref[...]

    heads = range(GDN_HB)
    ss = [s_scr[hg * GDN_HB + r] for r in heads]
    for ci in range(nch):
        rows = slice(ci * c, (ci + 1) * c)
        sb = [s_.astype(BF16) for s_ in ss]
        ds = [_dot(phi_scr[r, ci], sb[r]) for r in heads]
        ss = [ss[r] * egl_scr[r, ci, 0:1, :] + ds[r] + psi_scr[r, ci] for r in heads]
        os_ = [_dot(qp_scr[r, rows, :], sb[r]) + op_scr[r, rows, :] for r in heads]
        ons = [_rms(o) * nw for o in os_]
        for r in heads:
            cols = slice(r * dk, (r + 1) * dk)
            o_ref[rows, cols] = (ons[r] * z_ref[rows, cols].astype(F32)).astype(o_ref.dtype)
    for r in heads:
        s_scr[hg * GDN_HB + r] = ss[r]


def _gdn(qk, v, z, g_fields, g_rows, norm_w, batch, seq):
    bt = qk.shape[0]
    tbs = GDN_TB
    ntb = seq // tbs
    nhg = GDN_HEADS // GDN_HB
    cw = GDN_HB * GDN_DK
    width = GDN_HEADS * GDN_DK
    row = lambda b, t, h: b * ntb + t
    return pl.pallas_call(
        _gdn_kernel,
        grid=(batch, ntb, nhg),
        in_specs=[
            pl.BlockSpec((tbs, cw), lambda b, t, h: (row(b, t, h), h)),
            pl.BlockSpec((tbs, cw), lambda b, t, h: (row(b, t, h), nhg + h)),
            pl.BlockSpec((tbs, cw), lambda b, t, h: (row(b, t, h), h)),
            pl.BlockSpec((tbs, cw), lambda b, t, h: (row(b, t, h), h)),
            pl.BlockSpec((4, tbs, LANES), lambda b, t, h: (0, row(b, t, h), 0)),
            pl.BlockSpec((4, tbs // LANES, LANES, LANES), lambda b, t, h: (0, row(b, t, h), 0, 0)),
            pl.BlockSpec((1, GDN_DK), lambda b, t, h: (0, 0)),
        ],
        out_specs=pl.BlockSpec((tbs, cw), lambda b, t, h: (row(b, t, h), h)),
        out_shape=jax.ShapeDtypeStruct((bt, width), BF16),
        scratch_shapes=[
            pltpu.VMEM((GDN_HEADS, GDN_DK, GDN_DK), F32),
            pltpu.VMEM((GDN_HB, tbs, GDN_DK), BF16),
            pltpu.VMEM((GDN_HB, tbs, GDN_DK), F32),
            pltpu.VMEM((GDN_HB, tbs // GDN_CHUNK, GDN_DK, GDN_DK), BF16),
            pltpu.VMEM((GDN_HB, tbs // GDN_CHUNK, GDN_DK, GDN_DK), F32),
            pltpu.VMEM((GDN_HB, tbs // GDN_CHUNK, 8, GDN_DK), F32),
        ],
        compiler_params=pltpu.CompilerParams(
            dimension_semantics=("arbitrary", "arbitrary", "arbitrary"),
            vmem_limit_bytes=VMEM_LIMIT),
        name="gdn",
    )(qk, qk, v, z, g_fields, g_rows, norm_w)


def _ssd_kernel(x_ref, b_ref, c_ref, z_ref, sm_ref, smt_ref, dsk_ref, nw_ref,
                o_ref,
                h_scr):
    tb = pl.program_id(1)
    g = pl.program_id(2)
    tbs = x_ref.shape[0]
    l = SSM_CHUNK
    p = SSM_P
    nch = tbs // l

    @pl.when(tb == 0)
    def _():
        h_scr[g] = jnp.zeros(h_scr.shape[1:], F32)

    ii = lax.broadcasted_iota(jnp.int32, (l, l), 0)
    jj = lax.broadcasted_iota(jnp.int32, (l, l), 1)
    causal = ii >= jj
    lane_iota = lax.broadcasted_iota(jnp.int32, (l, LANES), 1)
    dsk = dsk_ref[...]
    nw = nw_ref[...]

    first_of_pair = lane_iota < p
    npair = SSM_R // 2
    cpi = SSD_CPI

    def pick(a0, a1):
        return jnp.where(first_of_pair[:a0.shape[0]], a0, a1)

    def body(it, carry):
        h = h_scr[g]
        st = []
        for cc_ in range(cpi):
            ci = it * cpi + cc_
            rows = pl.ds(pl.multiple_of(ci * l, l), l)
            bc = b_ref[rows, :]
            cm = c_ref[rows, :]
            sm = sm_ref[:, rows, :]
            rowf = [[smt_ref[f, ci, pl.ds(LANE_DT + g * SSM_R + r, 1), :] for f in range(4)]
                    for r in range(SSM_R)]
            st.append(dict(rows=rows, xb=x_ref[rows, :], cm=cm, sm=sm, rowf=rowf,
                           scores=_dot_nt(cm, bc), bct=bc.astype(F32).T))
        for s_ in st:
            s_["ac_c"] = [_col(s_["sm"][0], lane_iota, LANE_DT + g * SSM_R + r) for r in range(SSM_R)]
            s_["eac_c"] = [_col(s_["sm"][1], lane_iota, LANE_DT + g * SSM_R + r) for r in range(SSM_R)]
        for s_ in st:
            s_["m"] = [(s_["scores"] * jnp.where(
                causal, jnp.exp(jnp.where(causal, s_["ac_c"][r] - s_["rowf"][r][1], 0.0)), 0.0)
                * s_["rowf"][r][0]).astype(BF16) for r in range(SSM_R)]
            s_["bt"] = [(s_["bct"] * s_["rowf"][r][2]).astype(BF16) for r in range(SSM_R)]
        for s_ in st:
            yd, stt, eac, eal = [], [], [], []
            for pr in range(npair):
                xp = s_["xb"][:, pr * LANES:(pr + 1) * LANES]
                r0_, r1_ = 2 * pr, 2 * pr + 1
                yd.append(pick(_dot(s_["m"][r0_], xp), _dot(s_["m"][r1_], xp)))
                stt.append(pick(_dot(s_["bt"][r0_], xp), _dot(s_["bt"][r1_], xp)))
                eac.append(pick(jnp.broadcast_to(s_["eac_c"][r0_], (l, LANES)),
                                jnp.broadcast_to(s_["eac_c"][r1_], (l, LANES))))
                eal.append(pick(s_["rowf"][r0_][3], s_["rowf"][r1_][3]))
            s_["yd"] = jnp.concatenate(yd, axis=1)
            s_["states"] = jnp.concatenate(stt, axis=1)
            s_["eac"] = jnp.concatenate(eac, axis=1)
            s_["eal"] = jnp.concatenate(eal, axis=1)
        for s_ in st:
            y_off = _dot(s_["cm"], h.astype(BF16))
            h = h * s_["eal"] + s_["states"]
            y = (s_["yd"] + y_off * s_["eac"] + dsk * s_["xb"].astype(F32)) \
                * z_ref[s_["rows"], :].astype(F32)
            o_ref[s_["rows"], :] = (_rms(y) * nw).astype(o_ref.dtype)
        h_scr[g] = h
        return carry

    lax.fori_loop(0, nch // cpi, body, 0)


def _ssd(xbc, z, m_fields, m_rows, dskip, norm_w, batch, seq, z_col0):
    bt = xbc.shape[0]
    tbs = SSD_TB
    ntb = seq // tbs
    ng = SSM_GROUPS
    gw = SSM_R * SSM_P
    inner = SSM_HEADS * SSM_P
    bb0 = inner // SSM_N
    cb0 = (inner + ng * SSM_N) // SSM_N
    zb0 = z_col0 // gw
    row = lambda b, t, g: b * ntb + t
    return pl.pallas_call(
        _ssd_kernel,
        grid=(batch, ntb, ng),
        in_specs=[
            pl.BlockSpec((tbs, gw), lambda b, t, g: (row(b, t, g), g)),
            pl.BlockSpec((tbs, SSM_N), lambda b, t, g: (row(b, t, g), bb0 + g)),
            pl.BlockSpec((tbs, SSM_N), lambda b, t, g: (row(b, t, g), cb0 + g)),
            pl.BlockSpec((tbs, gw), lambda b, t, g: (row(b, t, g), zb0 + g)),
            pl.BlockSpec((2, tbs, LANES), lambda b, t, g: (0, row(b, t, g), 0)),
            pl.BlockSpec((4, tbs // SSM_CHUNK, LANES, SSM_CHUNK),
                         lambda b, t, g: (0, row(b, t, g), 0, 0)),
            pl.BlockSpec((1, gw), lambda b, t, g: (0, g)),
            pl.BlockSpec((1, gw), lambda b, t, g: (0, g)),
        ],
        out_specs=pl.BlockSpec((tbs, gw), lambda b, t, g: (row(b, t, g), g)),
        out_shape=jax.ShapeDtypeStruct((bt, inner), BF16),
        scratch_shapes=[pltpu.VMEM((ng, SSM_N, gw), F32)],
        compiler_params=pltpu.CompilerParams(
            dimension_semantics=("arbitrary", "arbitrary", "arbitrary"),
            vmem_limit_bytes=VMEM_LIMIT),
        name="ssd",
    )(xbc, xbc, xbc, z, m_fields, m_rows, dskip, norm_w)


def _merge_kernel(o_ref, y_ref, wg_ref, ws_ref, ga_ref, gb_ref, out_ref):
    a = _dot(o_ref[...], wg_ref[...])
    b = _dot(y_ref[...], ws_ref[...])
    out_ref[...] = (ga_ref[...].astype(F32) * a + gb_ref[...].astype(F32) * b).astype(out_ref.dtype)


def _merge(o, y, wg, ws, ga, gb, tm=512, tn=1024):
    bt, d_in = o.shape
    d = wg.shape[1]
    return pl.pallas_call(
        _merge_kernel,
        grid=(d // tn, bt // tm),
        in_specs=[
            pl.BlockSpec((tm, d_in), lambda j, i: (i, 0)),
            pl.BlockSpec((tm, d_in), lambda j, i: (i, 0)),
            pl.BlockSpec((d_in, tn), lambda j, i: (0, j)),
            pl.BlockSpec((d_in, tn), lambda j, i: (0, j)),
            pl.BlockSpec((tm, tn), lambda j, i: (i, j)),
            pl.BlockSpec((tm, tn), lambda j, i: (i, j)),
        ],
        out_specs=pl.BlockSpec((tm, tn), lambda j, i: (i, j)),
        out_shape=jax.ShapeDtypeStruct((bt, d), BF16),
        compiler_params=pltpu.CompilerParams(
            dimension_semantics=("parallel", "parallel"),
            vmem_limit_bytes=VMEM_LIMIT),
        name="merge",
    )(o, y, wg, ws, ga, gb)


def _oproj_kernel(m_ref, w_ref, h_ref, out_ref):
    out_ref[...] = h_ref[...] + _dot(m_ref[...], w_ref[...])


def _oproj(merged, wo, h, tm=1024, tn=1024):
    bt, d_in = merged.shape
    d = wo.shape[1]
    return pl.pallas_call(
        _oproj_kernel,
        grid=(d // tn, bt // tm),
        in_specs=[
            pl.BlockSpec((tm, d_in), lambda j, i: (i, 0)),
            pl.BlockSpec((d_in, tn), lambda j, i: (0, j)),
            pl.BlockSpec((tm, tn), lambda j, i: (i, j)),
        ],
        out_specs=pl.BlockSpec((tm, tn), lambda j, i: (i, j)),
        out_shape=jax.ShapeDtypeStruct((bt, d), F32),
        compiler_params=pltpu.CompilerParams(
            dimension_semantics=("parallel", "parallel"),
            vmem_limit_bytes=VMEM_LIMIT),
        name="oproj",
    )(merged, wo, h)


def _mlp_kernel(h_ref, nw_ref, wu_ref, wd_ref, fw_ref, out_ref, xn_scr, acc_scr, *, final):
    k = pl.program_id(1)

    @pl.when(k == 0)
    def _():
        xn_scr[...] = (_rms(h_ref[...]) * nw_ref[...]).astype(BF16)
        acc_scr[...] = jnp.zeros_like(acc_scr)

    up = _dot(xn_scr[...], wu_ref[...])
    act = jnp.square(jnp.maximum(up, 0.0)).astype(BF16)
    acc_scr[...] += _dot(act, wd_ref[...])

    @pl.when(k == pl.num_programs(1) - 1)
    def _():
        y = h_ref[...] + acc_scr[...]
        if final:
            y = _rms(y) * fw_ref[...]
        out_ref[...] = y


def _mlp(h, norm_w, w_up, w_down, final_w, final, tm=512, th=1024):
    bt, d = h.shape
    hid = w_up.shape[1]
    return pl.pallas_call(
        functools.partial(_mlp_kernel, final=final),
        grid=(bt // tm, hid // th),
        in_specs=[
            pl.BlockSpec((tm, d), lambda i, k: (i, 0)),
            pl.BlockSpec((1, d), lambda i, k: (0, 0)),
            pl.BlockSpec((d, th), lambda i, k: (0, k)),
            pl.BlockSpec((th, d), lambda i, k: (k, 0)),
            pl.BlockSpec((1, d), lambda i, k: (0, 0)),
        ],
        out_specs=pl.BlockSpec((tm, d), lambda i, k: (i, 0)),
        out_shape=jax.ShapeDtypeStruct((bt, d), F32),
        scratch_shapes=[pltpu.VMEM((tm, d), BF16), pltpu.VMEM((tm, d), F32)],
        compiler_params=pltpu.CompilerParams(
            dimension_semantics=("parallel", "arbitrary"),
            vmem_limit_bytes=VMEM_LIMIT),
        name="mlp",
    )(h, norm_w, w_up, w_down, final_w)


def kernel(x, norm1_w, w_in, gdn_conv_w, gdn_a_log, gdn_dt_bias, gdn_norm_w, ssm_conv_w, ssm_conv_b, ssm_a_log, ssm_dt_bias, ssm_d, ssm_norm_w, w_gdn_out, w_ssm_out, w_o, norm2_w, w_up, w_down, final_norm_w):
    batch, seq, d = x.shape
    depth = w_in.shape[0]
    assert depth >= 1
    bt = batch * seq
    gw = GDN_HEADS * GDN_DK
    inner = SSM_HEADS * SSM_P
    conv_ch = inner + 2 * SSM_GROUPS * SSM_N
    sizes = (3 * gw, gw, GDN_HEADS, GDN_HEADS, inner, conv_ch, SSM_HEADS, d, d)
    offs = [0]
    for s in sizes:
        offs.append(offs[-1] + s)
    o_qkv, o_gz, o_gb, o_ga, o_sz, o_xbc, o_dt, o_gate_a, o_gate_b, _ = offs
    no_w = jnp.zeros((CONV_K, max(gw, inner, d)), F32)
    no_b = jnp.zeros((1, max(2 * gw, conv_ch, d)), F32)
    zpad = jnp.zeros((LANES - 2 * GDN_HEADS - SSM_HEADS,), F32)

    h = x.reshape(bt, d)
    for l in range(depth):
        assert o_ga == o_gb + GDN_HEADS
        w_qk, w_v, w_xbc, w_gz, w_sz, w_ga, w_gb, w_small = _wprep(
            jnp.swapaxes(w_in[l], 0, 1),
            segs=((o_qkv, 2 * gw), (o_qkv + 2 * gw, gw), (o_xbc, conv_ch), (o_gz, gw), (o_sz, inner),
                  (o_gate_a, d), (o_gate_b, d)),
            smalls=((o_gb, 2 * GDN_HEADS), (o_dt, SSM_HEADS)))

        params = jnp.zeros((8, LANES), F32)
        params = params.at[0].set(jnp.concatenate(
            [jnp.zeros((GDN_HEADS,), F32), gdn_dt_bias[l], ssm_dt_bias[l], zpad]))
        params = params.at[1].set(jnp.concatenate(
            [jnp.zeros((GDN_HEADS,), F32), gdn_a_log[l], ssm_a_log[l], zpad]))
        xn, g_cols, g_rows, m_cols, m_rows = _smallprep(h, norm1_w[l][None, :], w_small, params)

        proj = functools.partial(_proj, xn, seq=seq)
        qk = proj(w_qk, gdn_conv_w[l][:, :2 * gw], no_b, conv=True, act="silu",
                  l2norm=True, q_tiles=gw // PROJ_TN, name="proj_qk")
        v = proj(w_v, gdn_conv_w[l][:, 2 * gw:], no_b, conv=True, act="silu",
                 l2norm=False, q_tiles=0, name="proj_v")
        xbc = proj(w_xbc, ssm_conv_w[l], ssm_conv_b[l][None, :], conv=True, act="silu",
                   l2norm=False, q_tiles=0, name="proj_xbc")
        gz = proj(w_gz, no_w, no_b, conv=False, act="silu", l2norm=False, q_tiles=0, name="proj_gz")
        sz = proj(w_sz, no_w, no_b, conv=False, act="silu", l2norm=False, q_tiles=0, name="proj_sz")
        ga = proj(w_ga, no_w, no_b, conv=False, act="sigmoid", l2norm=False, q_tiles=0,
                  name="proj_ga")
        gb = proj(w_gb, no_w, no_b, conv=False, act="sigmoid", l2norm=False, q_tiles=0,
                  name="proj_gb")

        o = _gdn(qk, v, gz, g_cols, g_rows, gdn_norm_w[l][None, :], batch, seq)
        dskip = jnp.repeat(ssm_d[l], SSM_P)[None, :]
        y = _ssd(xbc, sz, m_cols, m_rows, dskip, ssm_norm_w[l][None, :], batch, seq, 0)
        merged = _merge(o, y, w_gdn_out[l].astype(BF16), w_ssm_out[l].astype(BF16), ga, gb)
        h = _oproj(merged, w_o[l].astype(BF16), h)
        h = _mlp(h, norm2_w[l][None, :], w_up[l].astype(BF16), w_down[l].astype(BF16),
                 final_norm_w[None, :], final=(l == depth - 1))
    return h.reshape(batch, seq, d)
```

```python
import functools

import jax
import jax.numpy as jnp
from jax import lax
from jax.experimental import pallas as pl
from jax.experimental.pallas import tpu as pltpu

F32 = jnp.float32
BF16 = jnp.bfloat16
HIGHEST = lax.Precision.HIGHEST
EPS = 1e-6

LANES = 128
CONV_K = 4
GDN_HEADS = 16
GDN_DK = 128
GDN_CHUNK = 128
GDN_HB = 8
GDN_CPI = 4
SSM_HEADS = 32
SSM_P = 64
SSM_GROUPS = 8
SSM_N = 128
SSM_CHUNK = 128
SSM_R = SSM_HEADS // SSM_GROUPS
SSD_CPI = 4
GDN_TB = 512
SSD_TB = 1024
PROJ_TN = 2048
PROJ_SUB = 256
PROJ_ROWS = 64
PROJ_MSPLIT = 2
VMEM_LIMIT = 48 * 1024 * 1024

LANE_BETA = 0
LANE_GDEC = 16
LANE_DT = 32


def _sigmoid(x):
    return 0.5 * jnp.tanh(0.5 * x) + 0.5


def _silu(x):
    hx = 0.5 * x
    return hx * jnp.tanh(hx) + hx


def _dot(a, b, precision=None):
    return jnp.dot(a, b, preferred_element_type=F32, precision=precision)


def _dot3(a, b):
    ah = a.astype(BF16)
    al = (a - ah.astype(F32)).astype(BF16)
    bh = b.astype(BF16)
    bl = (b - bh.astype(F32)).astype(BF16)
    return _dot(ah, bh) + (_dot(ah, bl) + _dot(al, bh))


def _dot_nt(a, b):
    return lax.dot_general(a, b, (((1,), (1,)), ((), ())), preferred_element_type=F32)


def _dot_tn(a, b):
    return lax.dot_general(a, b, (((0,), (0,)), ((), ())), preferred_element_type=F32)


def _rms(x):
    return x * lax.rsqrt(jnp.mean(x * x, axis=-1, keepdims=True) + EPS)


def _wprep_kernel(w_ref, *o_refs, segs, smalls):
    for (a, n), o_ref in zip(segs, o_refs[:-1]):
        o_ref[...] = w_ref[a:a + n, :].astype(o_ref.dtype)
    parts = [w_ref[a:a + n, :] for a, n in smalls]
    parts.append(jnp.zeros((LANES - sum(n for _, n in smalls), w_ref.shape[1]), F32))
    o_refs[-1][...] = jnp.concatenate(parts, axis=0).astype(o_refs[-1].dtype)


def _wprep(wt, segs, smalls, tc=LANES):
    n_all, d = wt.shape
    heights = [n for _, n in segs] + [LANES]
    return pl.pallas_call(
        functools.partial(_wprep_kernel, segs=segs, smalls=smalls),
        grid=(d // tc,),
        in_specs=[pl.BlockSpec((n_all, tc), lambda i: (0, i))],
        out_specs=[pl.BlockSpec((n, tc), lambda i: (0, i)) for n in heights],
        out_shape=[jax.ShapeDtypeStruct((n, d), BF16) for n in heights],
        compiler_params=pltpu.CompilerParams(
            dimension_semantics=("parallel",), vmem_limit_bytes=VMEM_LIMIT),
        name="wprep",
    )(wt)


def _proj_kernel(xn_ref, w_ref, cw_ref, cb_ref, o_ref, *acc_scrs, conv, act, l2norm, q_tiles, seq):
    j = pl.program_id(0)
    i = pl.program_id(1)
    tm, tn = o_ref.shape
    nsub = tn // PROJ_SUB
    if conv:
        @pl.when((i * tm) % seq == 0)
        def _():
            for s in range(nsub):
                acc_scrs[s][0:8, :] = jnp.zeros((8, PROJ_SUB), F32)
    if l2norm:
        scale = jnp.where(j < q_tiles, GDN_DK ** -0.5, 1.0)
    z0 = pl.multiple_of(jnp.minimum(i, 0), 8)

    mh = tm // PROJ_MSPLIT

    def matmul(s, m):
        acc_scrs[s][8 + m * mh:8 + (m + 1) * mh, :] = _dot_nt(
            xn_ref[m * mh:(m + 1) * mh, :], w_ref[s * PROJ_SUB:(s + 1) * PROJ_SUB, :])

    def epilogue(s, m):
        acc_scr = acc_scrs[s]
        pieces = [(rc * PROJ_ROWS, lc * LANES)
                  for rc in range(m * mh // PROJ_ROWS, (m + 1) * mh // PROJ_ROWS)
                  for lc in range(PROJ_SUB // LANES)]
        for r0, l0 in pieces:
            sub = slice(l0, l0 + LANES)
            cols = slice(s * PROJ_SUB + l0, s * PROJ_SUB + l0 + LANES)
            if conv:
                win = acc_scr[pl.ds(z0 + r0, PROJ_ROWS + 8), sub]
                prev = pltpu.roll(win, 1, axis=0)
                lo = win * cw_ref[1:2, cols] + prev * cw_ref[0:1, cols]
                y = (win * cw_ref[3:4, cols] + prev * cw_ref[2:3, cols]
                     + pltpu.roll(lo, 2, axis=0))[8:] + cb_ref[:, cols]
            else:
                y = acc_scr[pl.ds(z0 + 8 + r0, PROJ_ROWS), sub]
            y = _silu(y) if act == "silu" else _sigmoid(y)
            if l2norm:
                y = y * (lax.rsqrt(jnp.sum(y * y, axis=-1, keepdims=True) + EPS) * scale)
            o_ref[r0:r0 + PROJ_ROWS, cols] = y.astype(o_ref.dtype)
        if conv and m == PROJ_MSPLIT - 1:
            acc_scr[0:8, :] = acc_scr[tm:tm + 8, :]

    units = [(s, m) for s in range(nsub) for m in range(PROJ_MSPLIT)]
    ahead = 2
    for u in units[:ahead]:
        matmul(*u)
    for n_, u in enumerate(units):
        if n_ + ahead < len(units):
            matmul(*units[n_ + ahead])
        epilogue(*u)


def _proj(xn, wt, cw, cb, *, conv, act, l2norm, q_tiles, seq, name, tm=1024, tn=PROJ_TN):
    bt, d = xn.shape
    n = wt.shape[0]
    assert seq % tm == 0 and n % tn == 0
    body = functools.partial(_proj_kernel, conv=conv, act=act, l2norm=l2norm, q_tiles=q_tiles, seq=seq)
    return pl.pallas_call(
        body,
        grid=(n // tn, bt // tm),
        in_specs=[
            pl.BlockSpec((tm, d), lambda j, i: (i, 0)),
            pl.BlockSpec((tn, d), lambda j, i: (j, 0)),
            pl.BlockSpec((CONV_K, tn), lambda j, i: (0, j)),
            pl.BlockSpec((1, tn), lambda j, i: (0, j)),
        ],
        out_specs=pl.BlockSpec((tm, tn), lambda j, i: (i, j)),
        out_shape=jax.ShapeDtypeStruct((bt, n), BF16),
        scratch_shapes=[pltpu.VMEM((tm + 8, PROJ_SUB), F32) for _ in range(tn // PROJ_SUB)],
        compiler_params=pltpu.CompilerParams(
            dimension_semantics=("arbitrary", "arbitrary"),
            vmem_limit_bytes=VMEM_LIMIT),
        name=name,
    )(xn, wt, cw, cb)


def _smallprep_kernel(x_ref, nw_ref, w_ref, p_ref, xn_ref, gc_ref, gr_ref, mc_ref, mr_ref):
    tm = x_ref.shape[0]
    xn_ref[...] = (_rms(x_ref[...]) * nw_ref[...]).astype(xn_ref.dtype)
    bias = p_ref[0:1, :]
    neg_a = -jnp.exp(p_ref[1:2, :])
    ii = lax.broadcasted_iota(jnp.int32, (LANES, LANES), 0)
    jj = lax.broadcasted_iota(jnp.int32, (LANES, LANES), 1)
    tril = jnp.where(ii >= jj, 1.0, 0.0).astype(F32)
    ones = jnp.ones((LANES, LANES), F32)
    logits = _dot_nt(xn_ref[...], w_ref[...])
    for s in range(tm // LANES):
        rows = slice(s * LANES, (s + 1) * LANES)
        x = logits[rows, :]
        xb = x + bias
        sp = jnp.maximum(xb, 0.0) + jnp.log1p(jnp.exp(-jnp.abs(xb)))
        dec = neg_a * sp
        cum = _dot(tril, dec, HIGHEST)
        last = _dot(ones, dec, HIGHEST)
        beta = _sigmoid(x)
        ecum = jnp.exp(cum)
        etail = jnp.exp(last - cum)
        elast = jnp.exp(last)
        for f, val in enumerate((beta, cum, ecum, etail)):
            gc_ref[f, rows, :] = val
        for f, val in enumerate((beta, cum, ecum, elast)):
            gr_ref[f, s] = val.T
        for f, val in enumerate((cum, ecum)):
            mc_ref[f, rows, :] = val
        for f, val in enumerate((sp, cum, sp * etail, elast)):
            mr_ref[f, s] = val.T


def _smallprep(x2, norm_w, w_small, params, tm=512):
    bt, d = x2.shape
    nb = tm // LANES

    def cspec(nf):
        return pl.BlockSpec((nf, tm, LANES), lambda i: (0, i, 0))

    def rspec(nf):
        return pl.BlockSpec((nf, nb, LANES, LANES), lambda i: (0, i, 0, 0))

    def cshape(nf):
        return jax.ShapeDtypeStruct((nf, bt, LANES), F32)

    def rshape(nf):
        return jax.ShapeDtypeStruct((nf, bt // LANES, LANES, LANES), F32)

    return pl.pallas_call(
        _smallprep_kernel,
        grid=(bt // tm,),
        in_specs=[pl.BlockSpec((tm, d), lambda i: (i, 0)),
                  pl.BlockSpec((1, d), lambda i: (0, 0)),
                  pl.BlockSpec((LANES, d), lambda i: (0, 0)),
                  pl.BlockSpec((8, LANES), lambda i: (0, 0))],
        out_specs=[pl.BlockSpec((tm, d), lambda i: (i, 0)), cspec(4), rspec(4), cspec(2), rspec(4)],
        out_shape=[jax.ShapeDtypeStruct((bt, d), BF16), cshape(4), rshape(4), cshape(2), rshape(4)],
        compiler_params=pltpu.CompilerParams(dimension_semantics=("parallel",)),
        name="smallprep",
    )(x2, norm_w, w_small, params)


def _col(field, lane_iota, lane):
    return jnp.sum(jnp.where(lane_iota == lane, field, 0.0), axis=-1, keepdims=True)


def _gdn_kernel(q_ref, k_ref, v_ref, z_ref, sm_ref, smt_ref, nw_ref,
                o_ref,
                s_scr, qp_scr, op_scr, phi_scr, psi_scr, egl_scr):
    tb = pl.program_id(1)
    hg = pl.program_id(2)
    tbs = q_ref.shape[0]
    nch = tbs // GDN_CHUNK
    c = GDN_CHUNK
    dk = GDN_DK

    @pl.when(tb == 0)
    def _():
        s_scr[pl.ds(hg * GDN_HB, GDN_HB)] = jnp.zeros((GDN_HB, dk, dk), F32)

    ii = lax.broadcasted_iota(jnp.int32, (c, c), 0)
    jj = lax.broadcasted_iota(jnp.int32, (c, c), 1)
    causal = ii >= jj
    strict = ii > jj
    nlev = 4
    blk = {w: (ii // w) == (jj // w) for w in (16, 32, 64, 128) if w < c}
    ring = {w: jnp.logical_and((ii // (2 * w)) == (jj // (2 * w)), (ii // w) != (jj // w))
            for w in blk}
    lane_iota = lax.broadcasted_iota(jnp.int32, (c, LANES), 1)
    q_lane = lax.broadcasted_iota(jnp.int32, (16, c), 1)
    q_blk = q_lane // 16
    q_eye = jnp.where(q_lane % 16 == lax.broadcasted_iota(jnp.int32, (16, c), 0), 1.0, 0.0).astype(F32)
    cpi = GDN_CPI

    def phase_a(it, carry):
        st = []
        for cc in range(cpi):
            ci = it * cpi + cc
            rows = pl.ds(pl.multiple_of(ci * c, c), c)
            sm = sm_ref[:, rows, :]
            for r in range(GDN_HB):
                h = hg * GDN_HB + r
                cols = slice(r * dk, (r + 1) * dk)
                gc_r, egl_r = (smt_ref[f, ci, pl.ds(LANE_GDEC + h, 1), :] for f in (1, 3))
                st.append(dict(
                    rows=rows, ci=ci, r=r, h=h, sm=sm, qb=q_ref[rows, cols], kb=k_ref[rows, cols],
                    vb=v_ref[rows, cols], gc_r=gc_r, egl=jnp.broadcast_to(egl_r, (8, dk))))
        grams = [_dot_nt(jnp.concatenate([s_["kb"], s_["qb"]], axis=0), s_["kb"]) for s_ in st]
        for f, name, lane0 in ((0, "beta_c", LANE_BETA), (1, "gc_c", LANE_GDEC),
                               (2, "egc_c", LANE_GDEC), (3, "ekl_c", LANE_GDEC)):
            for s_ in st:
                s_[name] = _col(s_["sm"][f], lane_iota, lane0 + s_["h"])
        decs = [jnp.where(causal, jnp.exp(jnp.where(causal, s_["gc_c"] - s_["gc_r"], 0.0)), 0.0)
                for s_ in st]
        for s_, gram, dec in zip(st, grams, decs):
            s_["a"] = jnp.where(strict, gram[:c] * s_["beta_c"] * dec, 0.0)
            s_["attn"] = (gram[c:] * dec).astype(BF16)
        for s_ in st:
            s_["qg"] = s_["qb"].astype(F32) * s_["egc_c"]
            s_["kd"] = (s_["kb"].astype(F32) * s_["ekl_c"]).astype(BF16)

        widths = sorted(blk)

        def expand(q):
            return jnp.where(blk[16], jnp.concatenate([q] * (c // 16), axis=0), 0.0)

        qm, qp = [], []
        for s_ in st:
            acc = None
            for b in range(c // 16):
                term = jnp.where(q_blk == b, s_["a"][16 * b:16 * (b + 1), :], 0.0)
                acc = term if acc is None else acc + term
            qm.append(-acc)
            qp.append(q_eye)
        for lev in range(nlev):
            wm = [expand(m_) for m_ in qm]
            if lev + 1 < nlev:
                outs = [_dot3(jnp.concatenate([m_, p_], axis=0), w_) for m_, p_, w_ in zip(qm, qp, wm)]
                qm = [o_[:16] for o_ in outs]
                qp = [p_ + o_[16:] for p_, o_ in zip(qp, outs)]
            else:
                qp = [p_ + _dot3(p_, w_) for p_, w_ in zip(qp, wm)]
        tinv = [expand(p_) for p_ in qp]
        for w_blk in widths:
            nb = c // (2 * w_blk)
            eb = [jnp.where(ring[w_blk], s_["a"], 0.0).astype(BF16) for s_ in st]
            low = [slice((2 * b + 1) * w_blk, (2 * b + 2) * w_blk) for b in range(nb)]
            upp = [slice(2 * b * w_blk, (2 * b + 1) * w_blk) for b in range(nb)]
            tl = [jnp.concatenate([t_[sl] for sl in low], axis=0) for t_ in tinv]
            tb_ = [t_.astype(BF16) for t_ in tinv]
            te = [_dot(l_.astype(BF16), e_).astype(BF16) for l_, e_ in zip(tl, eb)]
            tl = [l_ - _dot(e_, b_) for l_, e_, b_ in zip(tl, te, tb_)]
            tinv = [jnp.concatenate(
                [piece for b in range(nb) for piece in (t_[upp[b]], l_[b * w_blk:(b + 1) * w_blk])],
                axis=0) for t_, l_ in zip(tinv, tl)]

        rhs = [jnp.concatenate(
            [s_["kb"].astype(F32) * (s_["beta_c"] * s_["egc_c"]), s_["vb"].astype(F32) * s_["beta_c"]],
            axis=1).astype(BF16) for s_ in st]
        wus = [_dot(t_.astype(BF16), r_).astype(BF16) for t_, r_ in zip(tinv, rhs)]
        aws = [_dot(s_["attn"], wu_) for s_, wu_ in zip(st, wus)]
        kws = [_dot_tn(s_["kd"], wu_) for s_, wu_ in zip(st, wus)]
        for s_, aw, kw in zip(st, aws, kws):
            r, rows, ci = s_["r"], s_["rows"], s_["ci"]
            qp_scr[r, rows, :] = (s_["qg"] - aw[:, :dk]).astype(BF16)
            op_scr[r, rows, :] = aw[:, dk:]
            phi_scr[r, ci] = (-kw[:, :dk]).astype(BF16)
            psi_scr[r, ci] = kw[:, dk:]
            egl_scr[r, ci] = s_["egl"]
        return carry

    lax.fori_loop(0, nch // cpi, phase_a, 0)

    nw = nw_ref[...]

    heads = range(GDN_HB)
    ss = [s_scr[hg * GDN_HB + r] for r in heads]
    for ci in range(nch):
        rows = slice(ci * c, (ci + 1) * c)
        sb = [s_.astype(BF16) for s_ in ss]
        ds = [_dot(phi_scr[r, ci], sb[r]) for r in heads]
        ss = [ss[r] * egl_scr[r, ci, 0:1, :] + ds[r] + psi_scr[r, ci] for r in heads]
        os_ = [_dot(qp_scr[r, rows, :], sb[r]) + op_scr[r, rows, :] for r in heads]
        ons = [_rms(o) * nw for o in os_]
        for r in heads:
            cols = slice(r * dk, (r + 1) * dk)
            o_ref[rows, cols] = (ons[r] * z_ref[rows, cols].astype(F32)).astype(o_ref.dtype)
    for r in heads:
        s_scr[hg * GDN_HB + r] = ss[r]


def _gdn(qk, v, z, g_fields, g_rows, norm_w, batch, seq):
    bt = qk.shape[0]
    tbs = GDN_TB
    ntb = seq // tbs
    nhg = GDN_HEADS // GDN_HB
    cw = GDN_HB * GDN_DK
    width = GDN_HEADS * GDN_DK
    row = lambda b, t, h: b * ntb + t
    return pl.pallas_call(
        _gdn_kernel,
        grid=(batch, ntb, nhg),
        in_specs=[
            pl.BlockSpec((tbs, cw), lambda b, t, h: (row(b, t, h), h)),
            pl.BlockSpec((tbs, cw), lambda b, t, h: (row(b, t, h), nhg + h)),
            pl.BlockSpec((tbs, cw), lambda b, t, h: (row(b, t, h), h)),
            pl.BlockSpec((tbs, cw), lambda b, t, h: (row(b, t, h), h)),
            pl.BlockSpec((4, tbs, LANES), lambda b, t, h: (0, row(b, t, h), 0)),
            pl.BlockSpec((4, tbs // LANES, LANES, LANES), lambda b, t, h: (0, row(b, t, h), 0, 0)),
            pl.BlockSpec((1, GDN_DK), lambda b, t, h: (0, 0)),
        ],
        out_specs=pl.BlockSpec((tbs, cw), lambda b, t, h: (row(b, t, h), h)),
        out_shape=jax.ShapeDtypeStruct((bt, width), BF16),
        scratch_shapes=[
            pltpu.VMEM((GDN_HEADS, GDN_DK, GDN_DK), F32),
            pltpu.VMEM((GDN_HB, tbs, GDN_DK), BF16),
            pltpu.VMEM((GDN_HB, tbs, GDN_DK), F32),
            pltpu.VMEM((GDN_HB, tbs // GDN_CHUNK, GDN_DK, GDN_DK), BF16),
            pltpu.VMEM((GDN_HB, tbs // GDN_CHUNK, GDN_DK, GDN_DK), F32),
            pltpu.VMEM((GDN_HB, tbs // GDN_CHUNK, 8, GDN_DK), F32),
        ],
        compiler_params=pltpu.CompilerParams(
            dimension_semantics=("arbitrary", "arbitrary", "arbitrary"),
            vmem_limit_bytes=VMEM_LIMIT),
        name="gdn",
    )(qk, qk, v, z, g_fields, g_rows, norm_w)


def _ssd_kernel(x_ref, b_ref, c_ref, z_ref, sm_ref, smt_ref, dsk_ref, nw_ref,
                o_ref,
                h_scr):
    tb = pl.program_id(1)
    g = pl.program_id(2)
    tbs = x_ref.shape[0]
    l = SSM_CHUNK
    p = SSM_P
    nch = tbs // l

    @pl.when(tb == 0)
    def _():
        h_scr[g] = jnp.zeros(h_scr.shape[1:], F32)

    ii = lax.broadcasted_iota(jnp.int32, (l, l), 0)
    jj = lax.broadcasted_iota(jnp.int32, (l, l), 1)
    causal = ii >= jj
    lane_iota = lax.broadcasted_iota(jnp.int32, (l, LANES), 1)
    dsk = dsk_ref[...]
    nw = nw_ref[...]

    first_of_pair = lane_iota < p
    npair = SSM_R // 2
    cpi = SSD_CPI

    def pick(a0, a1):
        return jnp.where(first_of_pair[:a0.shape[0]], a0, a1)

    def body(it, carry):
        h = h_scr[g]
        st = []
        for cc_ in range(cpi):
            ci = it * cpi + cc_
            rows = pl.ds(pl.multiple_of(ci * l, l), l)
            bc = b_ref[rows, :]
            cm = c_ref[rows, :]
            sm = sm_ref[:, rows, :]
            rowf = [[smt_ref[f, ci, pl.ds(LANE_DT + g * SSM_R + r, 1), :] for f in range(4)]
                    for r in range(SSM_R)]
            st.append(dict(rows=rows, xb=x_ref[rows, :], cm=cm, sm=sm, rowf=rowf,
                           scores=_dot_nt(cm, bc), bct=bc.astype(F32).T))
        for s_ in st:
            s_["ac_c"] = [_col(s_["sm"][0], lane_iota, LANE_DT + g * SSM_R + r) for r in range(SSM_R)]
            s_["eac_c"] = [_col(s_["sm"][1], lane_iota, LANE_DT + g * SSM_R + r) for r in range(SSM_R)]
        for s_ in st:
            s_["m"] = [(s_["scores"] * jnp.where(
                causal, jnp.exp(jnp.where(causal, s_["ac_c"][r] - s_["rowf"][r][1], 0.0)), 0.0)
                * s_["rowf"][r][0]).astype(BF16) for r in range(SSM_R)]
            s_["bt"] = [(s_["bct"] * s_["rowf"][r][2]).astype(BF16) for r in range(SSM_R)]
        for s_ in st:
            yd, stt, eac, eal = [], [], [], []
            for pr in range(npair):
                xp = s_["xb"][:, pr * LANES:(pr + 1) * LANES]
                r0_, r1_ = 2 * pr, 2 * pr + 1
                yd.append(pick(_dot(s_["m"][r0_], xp), _dot(s_["m"][r1_], xp)))
                stt.append(pick(_dot(s_["bt"][r0_], xp), _dot(s_["bt"][r1_], xp)))
                eac.append(pick(jnp.broadcast_to(s_["eac_c"][r0_], (l, LANES)),
                                jnp.broadcast_to(s_["eac_c"][r1_], (l, LANES))))
                eal.append(pick(s_["rowf"][r0_][3], s_["rowf"][r1_][3]))
            s_["yd"] = jnp.concatenate(yd, axis=1)
            s_["states"] = jnp.concatenate(stt, axis=1)
            s_["eac"] = jnp.concatenate(eac, axis=1)
            s_["eal"] = jnp.concatenate(eal, axis=1)
        for s_ in st:
            y_off = _dot(s_["cm"], h.astype(BF16))
            h = h * s_["eal"] + s_["states"]
            y = (s_["yd"] + y_off * s_["eac"] + dsk * s_["xb"].astype(F32)) \
                * z_ref[s_["rows"], :].astype(F32)
            o_ref[s_["rows"], :] = (_rms(y) * nw).astype(o_ref.dtype)
        h_scr[g] = h
        return carry

    lax.fori_loop(0, nch // cpi, body, 0)


def _ssd(xbc, z, m_fields, m_rows, dskip, norm_w, batch, seq, z_col0):
    bt = xbc.shape[0]
    tbs = SSD_TB
    ntb = seq // tbs
    ng = SSM_GROUPS
    gw = SSM_R * SSM_P
    inner = SSM_HEADS * SSM_P
    bb0 = inner // SSM_N
    cb0 = (inner + ng * SSM_N) // SSM_N
    zb0 = z_col0 // gw
    row = lambda b, t, g: b * ntb + t
    return pl.pallas_call(
        _ssd_kernel,
        grid=(batch, ntb, ng),
        in_specs=[
            pl.BlockSpec((tbs, gw), lambda b, t, g: (row(b, t, g), g)),
            pl.BlockSpec((tbs, SSM_N), lambda b, t, g: (row(b, t, g), bb0 + g)),
            pl.BlockSpec((tbs, SSM_N), lambda b, t, g: (row(b, t, g), cb0 + g)),
            pl.BlockSpec((tbs, gw), lambda b, t, g: (row(b, t, g), zb0 + g)),
            pl.BlockSpec((2, tbs, LANES), lambda b, t, g: (0, row(b, t, g), 0)),
            pl.BlockSpec((4, tbs // SSM_CHUNK, LANES, SSM_CHUNK),
                         lambda b, t, g: (0, row(b, t, g), 0, 0)),
            pl.BlockSpec((1, gw), lambda b, t, g: (0, g)),
            pl.BlockSpec((1, gw), lambda b, t, g: (0, g)),
        ],
        out_specs=pl.BlockSpec((tbs, gw), lambda b, t, g: (row(b, t, g), g)),
        out_shape=jax.ShapeDtypeStruct((bt, inner), BF16),
        scratch_shapes=[pltpu.VMEM((ng, SSM_N, gw), F32)],
        compiler_params=pltpu.CompilerParams(
            dimension_semantics=("arbitrary", "arbitrary", "arbitrary"),
            vmem_limit_bytes=VMEM_LIMIT),
        name="ssd",
    )(xbc, xbc, xbc, z, m_fields, m_rows, dskip, norm_w)


def _merge_kernel(o_ref, y_ref, wg_ref, ws_ref, ga_ref, gb_ref, out_ref):
    a = _dot(o_ref[...], wg_ref[...])
    b = _dot(y_ref[...], ws_ref[...])
    out_ref[...] = (ga_ref[...].astype(F32) * a + gb_ref[...].astype(F32) * b).astype(out_ref.dtype)


def _merge(o, y, wg, ws, ga, gb, tm=512, tn=1024):
    bt, d_in = o.shape
    d = wg.shape[1]
    return pl.pallas_call(
        _merge_kernel,
        grid=(d // tn, bt // tm),
        in_specs=[
            pl.BlockSpec((tm, d_in), lambda j, i: (i, 0)),
            pl.BlockSpec((tm, d_in), lambda j, i: (i, 0)),
            pl.BlockSpec((d_in, tn), lambda j, i: (0, j)),
            pl.BlockSpec((d_in, tn), lambda j, i: (0, j)),
            pl.BlockSpec((tm, tn), lambda j, i: (i, j)),
            pl.BlockSpec((tm, tn), lambda j, i: (i, j)),
        ],
        out_specs=pl.BlockSpec((tm, tn), lambda j, i: (i, j)),
        out_shape=jax.ShapeDtypeStruct((bt, d), BF16),
        compiler_params=pltpu.CompilerParams(
            dimension_semantics=("parallel", "parallel"),
            vmem_limit_bytes=VMEM_LIMIT),
        name="merge",
    )(o, y, wg, ws, ga, gb)


def _oproj_kernel(m_ref, w_ref, h_ref, out_ref):
    out_ref[...] = h_ref[...] + _dot(m_ref[...], w_ref[...])


def _oproj(merged, wo, h, tm=1024, tn=1024):
    bt, d_in = merged.shape
    d = wo.shape[1]
    return pl.pallas_call(
        _oproj_kernel,
        grid=(d // tn, bt // tm),
        in_specs=[
            pl.BlockSpec((tm, d_in), lambda j, i: (i, 0)),
            pl.BlockSpec((d_in, tn), lambda j, i: (0, j)),
            pl.BlockSpec((tm, tn), lambda j, i: (i, j)),
        ],
        out_specs=pl.BlockSpec((tm, tn), lambda j, i: (i, j)),
        out_shape=jax.ShapeDtypeStruct((bt, d), F32),
        compiler_params=pltpu.CompilerParams(
            dimension_semantics=("parallel", "parallel"),
            vmem_limit_bytes=VMEM_LIMIT),
        name="oproj",
    )(merged, wo, h)


def _mlp_kernel(h_ref, nw_ref, wu_ref, wd_ref, fw_ref, out_ref, xn_scr, acc_scr, *, final):
    k = pl.program_id(1)

    @pl.when(k == 0)
    def _():
        xn_scr[...] = (_rms(h_ref[...]) * nw_ref[...]).astype(BF16)
        acc_scr[...] = jnp.zeros_like(acc_scr)

    up = _dot(xn_scr[...], wu_ref[...])
    act = jnp.square(jnp.maximum(up, 0.0)).astype(BF16)
    acc_scr[...] += _dot(act, wd_ref[...])

    @pl.when(k == pl.num_programs(1) - 1)
    def _():
        y = h_ref[...] + acc_scr[...]
        if final:
            y = _rms(y) * fw_ref[...]
        out_ref[...] = y


def _mlp(h, norm_w, w_up, w_down, final_w, final, tm=512, th=1024):
    bt, d = h.shape
    hid = w_up.shape[1]
    return pl.pallas_call(
        functools.partial(_mlp_kernel, final=final),
        grid=(bt // tm, hid // th),
        in_specs=[
            pl.BlockSpec((tm, d), lambda i, k: (i, 0)),
            pl.BlockSpec((1, d), lambda i, k: (0, 0)),
            pl.BlockSpec((d, th), lambda i, k: (0, k)),
            pl.BlockSpec((th, d), lambda i, k: (k, 0)),
            pl.BlockSpec((1, d), lambda i, k: (0, 0)),
        ],
        out_specs=pl.BlockSpec((tm, d), lambda i, k: (i, 0)),
        out_shape=jax.ShapeDtypeStruct((bt, d), F32),
        scratch_shapes=[pltpu.VMEM((tm, d), BF16), pltpu.VMEM((tm, d), F32)],
        compiler_params=pltpu.CompilerParams(
            dimension_semantics=("parallel", "arbitrary"),
            vmem_limit_bytes=VMEM_LIMIT),
        name="mlp",
    )(h, norm_w, w_up, w_down, final_w)


def kernel(x, norm1_w, w_in, gdn_conv_w, gdn_a_log, gdn_dt_bias, gdn_norm_w, ssm_conv_w, ssm_conv_b, ssm_a_log, ssm_dt_bias, ssm_d, ssm_norm_w, w_gdn_out, w_ssm_out, w_o, norm2_w, w_up, w_down, final_norm_w):
    batch, seq, d = x.shape
    depth = w_in.shape[0]
    assert depth >= 1
    bt = batch * seq
    gw = GDN_HEADS * GDN_DK
    inner = SSM_HEADS * SSM_P
    conv_ch = inner + 2 * SSM_GROUPS * SSM_N
    sizes = (3 * gw, gw, GDN_HEADS, GDN_HEADS, inner, conv_ch, SSM_HEADS, d, d)
    offs = [0]
    for s in sizes:
        offs.append(offs[-1] + s)
    o_qkv, o_gz, o_gb, o_ga, o_sz, o_xbc, o_dt, o_gate_a, o_gate_b, _ = offs
    no_w = jnp.zeros((CONV_K, max(gw, inner, d)), F32)
    no_b = jnp.zeros((1, max(2 * gw, conv_ch, d)), F32)
    zpad = jnp.zeros((LANES - 2 * GDN_HEADS - SSM_HEADS,), F32)

    h = x.reshape(bt, d)
    for l in range(depth):
        assert o_ga == o_gb + GDN_HEADS
        w_qk, w_v, w_xbc, w_gz, w_sz, w_ga, w_gb, w_small = _wprep(
            jnp.swapaxes(w_in[l], 0, 1),
            segs=((o_qkv, 2 * gw), (o_qkv + 2 * gw, gw), (o_xbc, conv_ch), (o_gz, gw), (o_sz, inner),
                  (o_gate_a, d), (o_gate_b, d)),
            smalls=((o_gb, 2 * GDN_HEADS), (o_dt, SSM_HEADS)))

        params = jnp.zeros((8, LANES), F32)
        params = params.at[0].set(jnp.concatenate(
            [jnp.zeros((GDN_HEADS,), F32), gdn_dt_bias[l], ssm_dt_bias[l], zpad]))
        params = params.at[1].set(jnp.concatenate(
            [jnp.zeros((GDN_HEADS,), F32), gdn_a_log[l], ssm_a_log[l], zpad]))
        xn, g_cols, g_rows, m_cols, m_rows = _smallprep(h, norm1_w[l][None, :], w_small, params)

        proj = functools.partial(_proj, xn, seq=seq)
        qk = proj(w_qk, gdn_conv_w[l][:, :2 * gw], no_b, conv=True, act="silu",
                  l2norm=True, q_tiles=gw // PROJ_TN, name="proj_qk")
        v = proj(w_v, gdn_conv_w[l][:, 2 * gw:], no_b, conv=True, act="silu",
                 l2norm=False, q_tiles=0, name="proj_v")
        xbc = proj(w_xbc, ssm_conv_w[l], ssm_conv_b[l][None, :], conv=True, act="silu",
                   l2norm=False, q_tiles=0, name="proj_xbc")
        gz = proj(w_gz, no_w, no_b, conv=False, act="silu", l2norm=False, q_tiles=0, name="proj_gz")
        sz = proj(w_sz, no_w, no_b, conv=False, act="silu", l2norm=False, q_tiles=0, name="proj_sz")
        ga = proj(w_ga, no_w, no_b, conv=False, act="sigmoid", l2norm=False, q_tiles=0,
                  name="proj_ga")
        gb = proj(w_gb, no_w, no_b, conv=False, act="sigmoid", l2norm=False, q_tiles=0,
                  name="proj_gb")

        o = _gdn(qk, v, gz, g_cols, g_rows, gdn_norm_w[l][None, :], batch, seq)
        dskip = jnp.repeat(ssm_d[l], SSM_P)[None, :]
        y = _ssd(xbc, sz, m_cols, m_rows, dskip, ssm_norm_w[l][None, :], batch, seq, 0)
        merged = _merge(o, y, w_gdn_out[l].astype(BF16), w_ssm_out[l].astype(BF16), ga, gb)
        h = _oproj(merged, w_o[l].astype(BF16), h)
        h = _mlp(h, norm2_w[l][None, :], w_up[l].astype(BF16), w_down[l].astype(BF16),
                 final_norm_w[None, :], final=(l == depth - 1))
    return h.reshape(batch, seq, d)
```
